```python
import math
import jax, jax.numpy as jnp
from jax import lax
import numpy as np

D_MODEL = 4096
BATCH = 2
SEQ = 8192
DEPTH = 2

RMS_EPS = 1e-6
GDN_HEADS = D_MODEL // 512
GDN_HEAD_DIM = 128
GDN_WIDTH = GDN_HEADS * GDN_HEAD_DIM
GDN_CONV_WIDTH = 4
GDN_CHUNK = 64
CONF_CHANNELS = D_MODEL // 4
CONF_KERNEL = 31
DIFF_HEADS = D_MODEL // 512
DIFF_QK_DIM = 64
DIFF_V_DIM = 2 * DIFF_QK_DIM
DIFF_QK_WIDTH = DIFF_HEADS * 2 * DIFF_QK_DIM
DIFF_V_WIDTH = DIFF_HEADS * DIFF_V_DIM
Q_BLOCK = 128
N_EXPERTS = 16
N_GROUPS = 4
TOP_K = 2
EXPERT_FF = D_MODEL // 4
ADA_CHUNKS = 6
IN_SIZES = (GDN_WIDTH, GDN_WIDTH, GDN_WIDTH, GDN_WIDTH, GDN_HEADS, GDN_HEADS,
            2 * CONF_CHANNELS,
            DIFF_QK_WIDTH, DIFF_QK_WIDTH, DIFF_V_WIDTH,
            D_MODEL, D_MODEL, D_MODEL)
IN_COLS = sum(IN_SIZES)

kernel_name = 'hybrid_gdn_conformer_diffattn_grouped_moe'


def rmsnorm(x, g):
    xf = x.astype(jnp.float32)
    y = xf * lax.rsqrt(jnp.mean(xf * xf, axis=-1, keepdims=True) + RMS_EPS)
    return (y * g.astype(jnp.float32)).astype(x.dtype)


def layernorm(x, g, b):
    xf = x.astype(jnp.float32)
    mu = jnp.mean(xf, axis=-1, keepdims=True)
    xc = xf - mu
    var = jnp.mean(xc * xc, axis=-1, keepdims=True)
    return (xc * lax.rsqrt(var + RMS_EPS) * g.astype(jnp.float32) + b.astype(jnp.float32)).astype(x.dtype)


def l2norm(x):
    xf = x.astype(jnp.float32)
    return xf * lax.rsqrt(jnp.sum(xf * xf, axis=-1, keepdims=True) + RMS_EPS)


def causal_depthwise_conv(x, w):
    k_w, ch = w.shape
    return lax.conv_general_dilated(
        x, w[:, None, :].astype(x.dtype), window_strides=(1,), padding=[(k_w - 1, 0)],
        dimension_numbers=('NWC', 'WIO', 'NWC'), feature_group_count=ch)


def split_columns(p):
    out = []
    start = 0
    for size in IN_SIZES:
        out.append(p[..., start:start + size])
        start += size
    return out


def alibi_slopes(n_heads):
    return 2.0 ** (-8.0 * jnp.arange(1, n_heads + 1, dtype=jnp.float32) / n_heads)


def chunked_gated_delta_rule(q, k, v, g, beta):
    B, T, H, Dk = q.shape
    Dv = v.shape[-1]
    C = GDN_CHUNK
    N = T // C
    f32 = jnp.float32

    def chunks(t):
        return t.astype(f32).reshape(B, N, C, H, -1).transpose(0, 3, 1, 2, 4)

    q = chunks(q) * (Dk ** -0.5)
    k = chunks(k)
    v = chunks(v)
    g = jnp.cumsum(chunks(g[..., None])[..., 0], axis=-1)
    beta = chunks(beta[..., None])[..., 0]
    causal = jnp.tril(jnp.ones((C, C), bool))
    strict = jnp.tril(jnp.ones((C, C), bool), -1)
    gdiff = g[..., :, None] - g[..., None, :]
    decay = jnp.where(causal, jnp.exp(jnp.where(causal, gdiff, 0.0)), 0.0)
    k_beta = k * beta[..., None]
    lower = jnp.where(strict, jnp.einsum('bhnid,bhnjd->bhnij', k_beta, k) * decay, 0.0)
    eye = jnp.eye(C, dtype=f32)
    t_inv = lax.linalg.triangular_solve(eye + lower, jnp.broadcast_to(eye, lower.shape),
                                        left_side=True, lower=True, unit_diagonal=True)
    u = jnp.einsum('bhnij,bhnjd->bhnid', t_inv, v * beta[..., None])
    w = jnp.einsum('bhnij,bhnjd->bhnid', t_inv, k_beta * jnp.exp(g)[..., None])
    qk = jnp.einsum('bhnid,bhnjd->bhnij', q, k) * decay
    q_dec = q * jnp.exp(g)[..., None]
    k_tail = k * jnp.exp(g[..., -1:] - g)[..., None]
    chunk_decay = jnp.exp(g[..., -1])

    def step(S, xs):
        q_i, k_i, u_i, w_i, qk_i, d_i = xs
        v_new = u_i - jnp.einsum('bhcd,bhde->bhce', w_i, S)
        o_i = jnp.einsum('bhcd,bhde->bhce', q_i, S) + jnp.einsum('bhij,bhje->bhie', qk_i, v_new)
        S = S * d_i[..., None, None] + jnp.einsum('bhcd,bhce->bhde', k_i, v_new)
        return S, o_i

    xs = tuple(jnp.moveaxis(t, 2, 0) for t in (q_dec, k_tail, u, w, qk, chunk_decay))
    S0 = jnp.zeros((B, H, Dk, Dv), f32)
    _, o = lax.scan(step, S0, xs)
    return o.transpose(1, 0, 3, 2, 4).reshape(B, T, H, Dv)


def gdn_mixer(q, k, v, z, b, a, conv_w, a_log, dt_bias, norm_g, w_out):
    B, T, _ = q.shape
    H, hd = GDN_HEADS, GDN_HEAD_DIM
    qkv = jax.nn.silu(causal_depthwise_conv(jnp.concatenate([q, k, v], axis=-1), conv_w))
    q, k, v = qkv[..., :GDN_WIDTH], qkv[..., GDN_WIDTH:2 * GDN_WIDTH], qkv[..., 2 * GDN_WIDTH:]
    q = l2norm(q.reshape(B, T, H, hd))
    k = l2norm(k.reshape(B, T, H, hd))
    v = v.reshape(B, T, H, hd)
    beta = jax.nn.sigmoid(b.astype(jnp.float32))
    g = -jnp.exp(a_log.astype(jnp.float32)) * jax.nn.softplus(a.astype(jnp.float32) + dt_bias.astype(jnp.float32))
    o = chunked_gated_delta_rule(q, k, v, g, beta)
    o = rmsnorm(o, norm_g) * jax.nn.silu(z.reshape(B, T, H, hd).astype(jnp.float32))
    return o.reshape(B, T, GDN_WIDTH) @ w_out


def conformer_conv_mixer(u, dw_w, dw_b, ln_g, ln_b, w_out):
    h = jax.nn.glu(u, axis=-1)
    h = causal_depthwise_conv(h, dw_w) + dw_b
    h = jax.nn.silu(layernorm(h, ln_g, ln_b))
    return h @ w_out


def diff_attention_mixer(q, k, v, q_norm_g, k_norm_g, lam_q1, lam_k1, lam_q2, lam_k2,
                         sub_g, w_out, lambda_init):
    B, T, _ = q.shape
    H, d, dv = DIFF_HEADS, DIFF_QK_DIM, DIFF_V_DIM
    f32 = jnp.float32
    q = rmsnorm(q.reshape(B, T, H, 2, d), q_norm_g).astype(f32) * (d ** -0.5)
    k = rmsnorm(k.reshape(B, T, H, 2, d), k_norm_g).astype(f32)
    v = v.reshape(B, T, H, dv).astype(f32)
    lam = (jnp.exp(jnp.sum(lam_q1.astype(f32) * lam_k1.astype(f32)))
           - jnp.exp(jnp.sum(lam_q2.astype(f32) * lam_k2.astype(f32))) + lambda_init)
    slopes = alibi_slopes(H)
    kpos = jnp.arange(T)
    n_blocks = T // Q_BLOCK
    q_blocks = jnp.moveaxis(q.reshape(B, n_blocks, Q_BLOCK, H, 2, d), 1, 0)

    def attend_block(args):
        q_blk, blk = args
        qpos = blk * Q_BLOCK + jnp.arange(Q_BLOCK)
        dist = (qpos[:, None] - kpos[None, :]).astype(f32)
        bias = -slopes[:, None, None] * dist
        s = jnp.einsum('bqhcd,bkhcd->bhcqk', q_blk, k) + bias[None, :, None]
        s = jnp.where(dist >= 0, s, -jnp.inf)
        p = jax.nn.softmax(s, axis=-1)
        attn = p[:, :, 0] - lam * p[:, :, 1]
        return jnp.einsum('bhqk,bkhe->bqhe', attn, v)

    o = lax.map(attend_block, (q_blocks, jnp.arange(n_blocks)))
    o = jnp.moveaxis(o, 0, 1).reshape(B, T, H, dv)
    o = rmsnorm(o, sub_g) * (1.0 - lambda_init)
    return o.reshape(B, T, DIFF_V_WIDTH) @ w_out


def grouped_moe(h, router_w, router_bias, w_gate, w_up, w_down):
    B, T, D = h.shape
    f32 = jnp.float32
    xt = h.reshape(B * T, D)
    per_group = N_EXPERTS // N_GROUPS
    scores = jax.nn.sigmoid((xt @ router_w).astype(f32))
    sel = scores + router_bias.astype(f32)
    group_score = lax.top_k(sel.reshape(-1, N_GROUPS, per_group), TOP_K)[0].sum(-1)
    best_group = jnp.argmax(group_score, axis=-1)
    in_group = (jnp.arange(N_EXPERTS) // per_group)[None, :] == best_group[:, None]
    _, idx = lax.top_k(jnp.where(in_group, sel, -jnp.inf), TOP_K)
    wts = jnp.take_along_axis(scores, idx, axis=-1)
    wts = wts / jnp.sum(wts, axis=-1, keepdims=True)
    combine = jnp.sum(jax.nn.one_hot(idx, N_EXPERTS, dtype=f32) * wts[..., None], axis=1)
    y = jnp.zeros((B * T, D), f32)
    for e in range(N_EXPERTS):
        hid = jax.nn.silu(xt @ w_gate[e]) * (xt @ w_up[e])
        y = y + combine[:, e:e + 1] * (hid @ w_down[e])
    return y.reshape(B, T, D).astype(h.dtype)


def setup_inputs(seed: int = 0) -> dict:
    key = jax.random.key(seed)
    ks = jax.random.split(key, 31)
    L = DEPTH
    f32 = jnp.float32

    def nrm(i, shape, scale):
        return jax.random.normal(ks[i], shape, f32) * scale

    def gain(i, shape):
        return 1.0 + nrm(i, shape, 0.05)

    return {
        'x': nrm(0, (BATCH, SEQ, D_MODEL), 1.0),
        'c': nrm(1, (BATCH, D_MODEL), 1.0),
        'ada_w': nrm(2, (L, D_MODEL, ADA_CHUNKS * D_MODEL), 0.5 * D_MODEL ** -0.5),
        'ada_b': nrm(3, (L, ADA_CHUNKS * D_MODEL), 0.02),
        'norm1_g': gain(4, (L, D_MODEL)),
        'w_in': nrm(5, (L, D_MODEL, IN_COLS), D_MODEL ** -0.5),
        'gdn_conv_w': nrm(6, (L, GDN_CONV_WIDTH, 3 * GDN_WIDTH), GDN_CONV_WIDTH ** -0.5),
        'gdn_a_log': jnp.log(jax.random.uniform(ks[7], (L, GDN_HEADS), f32, 1.0, 16.0)),
        'gdn_dt_bias': nrm(8, (L, GDN_HEADS), 0.1),
        'gdn_norm_g': gain(9, (L, GDN_HEAD_DIM)),
        'gdn_w_out': nrm(10, (L, GDN_WIDTH, D_MODEL), GDN_WIDTH ** -0.5),
        'conf_dw_w': nrm(11, (L, CONF_KERNEL, CONF_CHANNELS), CONF_KERNEL ** -0.5),
        'conf_dw_b': nrm(12, (L, CONF_CHANNELS), 0.02),
        'conf_ln_g': gain(13, (L, CONF_CHANNELS)),
        'conf_ln_b': nrm(14, (L, CONF_CHANNELS), 0.02),
        'conf_w_out': nrm(15, (L, CONF_CHANNELS, D_MODEL), CONF_CHANNELS ** -0.5),
        'diff_q_norm_g': gain(16, (L, DIFF_QK_DIM)),
        'diff_k_norm_g': gain(17, (L, DIFF_QK_DIM)),
        'diff_lambda_q1': nrm(18, (L, DIFF_QK_DIM), 0.1),
        'diff_lambda_k1': nrm(19, (L, DIFF_QK_DIM), 0.1),
        'diff_lambda_q2': nrm(20, (L, DIFF_QK_DIM), 0.1),
        'diff_lambda_k2': nrm(21, (L, DIFF_QK_DIM), 0.1),
        'diff_sub_g': gain(22, (L, DIFF_V_DIM)),
        'diff_w_out': nrm(23, (L, DIFF_V_WIDTH, D_MODEL), DIFF_V_WIDTH ** -0.5),
        'w_o': nrm(24, (L, D_MODEL, D_MODEL), D_MODEL ** -0.5),
        'norm2_g': gain(25, (L, D_MODEL)),
        'router_w': nrm(26, (D_MODEL, N_EXPERTS), D_MODEL ** -0.5),
        'router_bias': nrm(27, (N_EXPERTS,), 0.01),
        'exp_w_gate': nrm(28, (L, N_EXPERTS, D_MODEL, EXPERT_FF), D_MODEL ** -0.5),
        'exp_w_up': nrm(29, (L, N_EXPERTS, D_MODEL, EXPERT_FF), D_MODEL ** -0.5),
        'exp_w_down': nrm(30, (L, N_EXPERTS, EXPERT_FF, D_MODEL), EXPERT_FF ** -0.5),
    }


def reference(x, c, ada_w, ada_b, norm1_g, w_in, gdn_conv_w, gdn_a_log, gdn_dt_bias,
              gdn_norm_g, gdn_w_out, conf_dw_w, conf_dw_b, conf_ln_g, conf_ln_b, conf_w_out,
              diff_q_norm_g, diff_k_norm_g, diff_lambda_q1, diff_lambda_k1, diff_lambda_q2,
              diff_lambda_k2, diff_sub_g, diff_w_out, w_o, norm2_g, router_w, router_bias,
              exp_w_gate, exp_w_up, exp_w_down):
    cond = jax.nn.silu(c)
    for l in range(DEPTH):
        mod = cond @ ada_w[l] + ada_b[l]
        shift1, scale1, gate1, shift2, scale2, gate2 = [m[:, None, :] for m in jnp.split(mod, ADA_CHUNKS, axis=-1)]
        h = rmsnorm(x, norm1_g[l]) * (1.0 + scale1) + shift1
        (g_q, g_k, g_v, g_z, g_b, g_a, conf_in, d_q, d_k, d_v,
         gate_a, gate_b, gate_c) = split_columns(h @ w_in[l])
        y_a = gdn_mixer(g_q, g_k, g_v, g_z, g_b, g_a, gdn_conv_w[l], gdn_a_log[l], gdn_dt_bias[l],
                        gdn_norm_g[l], gdn_w_out[l])
        y_b = conformer_conv_mixer(conf_in, conf_dw_w[l], conf_dw_b[l], conf_ln_g[l], conf_ln_b[l],
                                   conf_w_out[l])
        lambda_init = 0.8 - 0.6 * math.exp(-0.3 * l)
        y_c = diff_attention_mixer(d_q, d_k, d_v, diff_q_norm_g[l], diff_k_norm_g[l],
                                   diff_lambda_q1[l], diff_lambda_k1[l], diff_lambda_q2[l],
                                   diff_lambda_k2[l], diff_sub_g[l], diff_w_out[l], lambda_init)
        mixed = (jax.nn.sigmoid(gate_a) * y_a + jax.nn.sigmoid(gate_b) * y_b
                 + jax.nn.sigmoid(gate_c) * y_c)
        x = x + gate1 * (mixed @ w_o[l])
        h2 = rmsnorm(x, norm2_g[l]) * (1.0 + scale2) + shift2
        x = x + gate2 * grouped_moe(h2, router_w, router_bias, exp_w_gate[l], exp_w_up[l], exp_w_down[l])
    return x
```

```python
import functools
import math

import jax
import jax.numpy as jnp
from jax import lax
from jax.experimental import pallas as pl
from jax.experimental.pallas import tpu as pltpu

F32 = jnp.float32
BF16 = jnp.bfloat16
HIGHEST = lax.Precision.HIGHEST

RMS_EPS = 1e-6
LANES = 128
GDN_HEAD_DIM = 128
GDN_CONV_WIDTH = 4
GDN_CHUNK = 64
CONF_KERNEL = 31
DIFF_QK_DIM = 64
DIFF_V_DIM = 128
N_EXPERTS = 16
N_GROUPS = 4
ADA_CHUNKS = 6
D_PER_HEAD = 512
VMEM_LIMIT_BYTES = 56 * 1024 * 1024


def _params(*semantics):
    return pltpu.CompilerParams(dimension_semantics=semantics,
                                vmem_limit_bytes=VMEM_LIMIT_BYTES)


def _sigmoid(x):
    return 1.0 / (1.0 + jnp.exp(-x))


def _silu(x):
    return x * _sigmoid(x)


def _softplus(x):
    return jnp.maximum(x, 0.0) + jnp.log(1.0 + jnp.exp(-jnp.abs(x)))


def _dot(a, b):
    return jnp.dot(a, b, preferred_element_type=F32)


def _dot_nt(a, b):
    return lax.dot_general(a, b, (((1,), (1,)), ((), ())), preferred_element_type=F32)


def _ada_kernel(c_ref, w_ref, b_ref, o_ref):
    cond = _silu(c_ref[...]).astype(BF16)
    o_ref[...] = _dot(cond, w_ref[...].astype(BF16)) + b_ref[...]


def ada_modulation(c, ada_w, ada_b):
    n_layers, d, d6 = ada_w.shape
    b = c.shape[0]
    rows = 8
    c_pad = jnp.zeros((rows, d), F32).at[:b].set(c)
    tn = min(512, d6)
    out = pl.pallas_call(
        _ada_kernel,
        grid=(n_layers, d6 // tn),
        in_specs=[pl.BlockSpec((rows, d), lambda l, j: (0, 0)),
                  pl.BlockSpec((None, d, tn), lambda l, j: (l, 0, j)),
                  pl.BlockSpec((None, 1, tn), lambda l, j: (l, 0, j))],
        out_specs=pl.BlockSpec((None, rows, tn), lambda l, j: (l, 0, j)),
        out_shape=jax.ShapeDtypeStruct((n_layers, rows, d6), F32),
        compiler_params=_params("parallel", "parallel"),
    )(c_pad, ada_w, ada_b.reshape(n_layers, 1, d6))
    return out[:, :b]


def _norm_mod(x, g, scale, shift):
    ms = jnp.mean(x * x, axis=-1, keepdims=True)
    y = x * lax.rsqrt(ms + RMS_EPS) * g
    return y * (1.0 + scale) + shift


def _normmod_kernel(x_ref, g_ref, sc_ref, sh_ref, h_ref):
    h_ref[...] = _norm_mod(x_ref[...], g_ref[...], sc_ref[...], sh_ref[...]).astype(h_ref.dtype)


def _resid_normmod_kernel(x_ref, y_ref, gate_ref, g_ref, sc_ref, sh_ref, xo_ref, h_ref):
    x = x_ref[...] + gate_ref[...] * y_ref[...].astype(F32)
    xo_ref[...] = x
    h_ref[...] = _norm_mod(x, g_ref[...], sc_ref[...], sh_ref[...]).astype(h_ref.dtype)


def _resid_kernel(x_ref, y_ref, gate_ref, xo_ref):
    xo_ref[...] = x_ref[...] + gate_ref[...] * y_ref[...].astype(F32)


def _mod_spec(d, tiles_per_batch, chunk):
    return pl.BlockSpec((None, None, 1, d), lambda i: (i // tiles_per_batch, chunk, 0, 0))


def norm_modulate(x, g, mod4, scale_chunk, shift_chunk, seq, y=None, gate_mod4=None, gate_chunk=None):
    n, d = x.shape
    tm = min(256, seq)
    tpb = seq // tm
    row = pl.BlockSpec((tm, d), lambda i: (i, 0))
    vec = pl.BlockSpec((1, d), lambda i: (0, 0))
    h_shape = jax.ShapeDtypeStruct((n, d), BF16)
    if y is None:
        return pl.pallas_call(
            _normmod_kernel, grid=(n // tm,),
            in_specs=[row, vec, _mod_spec(d, tpb, scale_chunk), _mod_spec(d, tpb, shift_chunk)],
            out_specs=row, out_shape=h_shape, compiler_params=_params("parallel"),
        )(x, g.reshape(1, d), mod4, mod4)
    return pl.pallas_call(
        _resid_normmod_kernel, grid=(n // tm,),
        in_specs=[row, row, _mod_spec(d, tpb, gate_chunk), vec,
                  _mod_spec(d, tpb, scale_chunk), _mod_spec(d, tpb, shift_chunk)],
        out_specs=[row, row],
        out_shape=[jax.ShapeDtypeStruct((n, d), F32), h_shape],
        compiler_params=_params("parallel"),
    )(x, y, gate_mod4, g.reshape(1, d), mod4, mod4)


def gated_residual(x, y, mod4, gate_chunk, seq):
    n, d = x.shape
    tm = min(256, seq)
    tpb = seq // tm
    row = pl.BlockSpec((tm, d), lambda i: (i, 0))
    return pl.pallas_call(
        _resid_kernel, grid=(n // tm,),
        in_specs=[row, row, _mod_spec(d, tpb, gate_chunk)],
        out_specs=row, out_shape=jax.ShapeDtypeStruct((n, d), F32),
        compiler_params=_params("parallel"),
    )(x, y, mod4)


def _mm_kernel(a_ref, b_ref, o_ref):
    o_ref[...] = _dot(a_ref[...], b_ref[...]).astype(o_ref.dtype)


def matmul(a, b, out_dtype, tm=1024, tn=512):
    m, k = a.shape
    n = b.shape[1]
    tm = min(tm, m)
    tn = min(tn, n)
    while n % tn:
        tn -= LANES
    return pl.pallas_call(
        _mm_kernel, grid=(m // tm, n // tn),
        in_specs=[pl.BlockSpec((tm, k), lambda i, j: (i, 0)),
                  pl.BlockSpec((k, tn), lambda i, j: (0, j))],
        out_specs=pl.BlockSpec((tm, tn), lambda i, j: (i, j)),
        out_shape=jax.ShapeDtypeStruct((m, n), out_dtype),
        compiler_params=_params("parallel", "parallel"),
    )(a, b)


def _unit_lower_inverse(lower, ii, jj):
    n = lower.shape[0]
    t = jnp.broadcast_to((ii == jj).astype(F32)[None], lower.shape)
    s = 1
    while s < GDN_CHUNK:
        sh = s.bit_length() - 1
        sel = ((ii >> (sh + 1)) == (jj >> (sh + 1))) & ((ii >> sh) != (jj >> sh)) & (ii > jj)
        lo = jnp.where(sel[None], lower, 0.0)
        tl = jnp.einsum('nij,njk->nik', t, lo, precision=HIGHEST, preferred_element_type=F32)
        t = t - jnp.einsum('nij,njk->nik', tl, t, precision=HIGHEST, preferred_element_type=F32)
        s *= 2
    del n
    return t


def _gdn_prep_kernel(q_ref, k_ref, v_ref, ba_ref, cwq_ref, cwk_ref, cwv_ref, alog_ref, dt_ref,
                     u_ref, w_ref, qd_ref, kt_ref, qk_ref, dec_ref,
                     xq_ref, xk_ref, xv_ref, *, heads, tb):
    c = GDN_CHUNK
    nc = tb // c
    halo = 8

    @pl.when(pl.program_id(1) == 0)
    def _():
        for xe in (xq_ref, xk_ref, xv_ref):
            xe[0:halo, :] = jnp.zeros((halo, xe.shape[1]), F32)

    def conv_silu(x_ref, xe, cw_ref):
        xe[halo:halo + tb, :] = x_ref[...].astype(F32)
        first = halo - (GDN_CONV_WIDTH - 1)
        acc = cw_ref[0:1, :] * xe[pl.ds(first, tb), :]
        for j in range(1, GDN_CONV_WIDTH):
            acc = acc + cw_ref[j:j + 1, :] * xe[pl.ds(first + j, tb), :]
        xe[0:halo, :] = xe[tb:tb + halo, :]
        return _silu(acc)

    qa = conv_silu(q_ref, xq_ref, cwq_ref)
    ka = conv_silu(k_ref, xk_ref, cwk_ref)
    va = conv_silu(v_ref, xv_ref, cwv_ref)

    ba = ba_ref[...]
    g_all = -jnp.exp(alog_ref[...]) * _softplus(ba + dt_ref[...])
    beta_all = _sigmoid(ba).reshape(nc, c, LANES)

    ii = lax.broadcasted_iota(jnp.int32, (c, c), 0)
    jj = lax.broadcasted_iota(jnp.int32, (c, c), 1)
    tril = ii >= jj
    strict = ii > jj
    eye = ii == jj
    tril_f = jnp.broadcast_to(tril.astype(F32)[None], (nc, c, c))
    ones_f = jnp.ones((nc, c, c), F32)
    gc_all = jnp.einsum('nij,njl->nil', tril_f, g_all.reshape(nc, c, LANES),
                        precision=HIGHEST, preferred_element_type=F32)

    for h in range(heads):
        sl = slice(h * GDN_HEAD_DIM, (h + 1) * GDN_HEAD_DIM)
        qh = qa[:, sl]
        kh = ka[:, sl]
        qh = qh * lax.rsqrt(jnp.sum(qh * qh, axis=-1, keepdims=True) + RMS_EPS) * (GDN_HEAD_DIM ** -0.5)
        kh = kh * lax.rsqrt(jnp.sum(kh * kh, axis=-1, keepdims=True) + RMS_EPS)
        q3 = qh.reshape(nc, c, GDN_HEAD_DIM)
        k3 = kh.reshape(nc, c, GDN_HEAD_DIM)
        v3 = va[:, sl].reshape(nc, c, GDN_HEAD_DIM)
        gcol = gc_all[:, :, heads + h:heads + h + 1]
        bcol = beta_all[:, :, h:h + 1]
        gcol_b = jnp.broadcast_to(gcol, (nc, c, c))
        grow_b = jnp.einsum('nim,nmj->nij', ones_f, jnp.where(eye[None], gcol_b, 0.0),
                            precision=HIGHEST, preferred_element_type=F32)
        decay = jnp.where(tril[None], jnp.exp(jnp.where(tril[None], gcol_b - grow_b, 0.0)), 0.0)
        glast = gcol[:, c - 1:c, :]
        eg = jnp.exp(gcol)
        kb = k3 * bcol
        k3b = k3.astype(BF16)
        kk = jnp.einsum('nid,njd->nij', kb.astype(BF16), k3b, preferred_element_type=F32)
        lower = jnp.where(strict[None], kk * decay, 0.0)
        t_inv = _unit_lower_inverse(lower, ii, jj).astype(BF16)
        u = jnp.einsum('nij,njd->nid', t_inv, (v3 * bcol).astype(BF16), preferred_element_type=F32)
        w = jnp.einsum('nij,njd->nid', t_inv, (kb * eg).astype(BF16), preferred_element_type=F32)
        qk = jnp.einsum('nid,njd->nij', q3.astype(BF16), k3b, preferred_element_type=F32) * decay
        u_ref[:, sl] = u.reshape(tb, GDN_HEAD_DIM)
        w_ref[:, sl] = w.reshape(tb, GDN_HEAD_DIM).astype(BF16)
        qd_ref[:, sl] = (q3 * eg).reshape(tb, GDN_HEAD_DIM).astype(BF16)
        kt_ref[:, sl] = (k3 * jnp.exp(glast - gcol)).reshape(tb, GDN_HEAD_DIM).astype(BF16)
        qk_pad = jnp.concatenate([qk, jnp.zeros_like(qk)], axis=-1)
        qk_ref[:, sl] = qk_pad.reshape(tb, GDN_HEAD_DIM).astype(BF16)
        dec_ref[:, sl] = jnp.broadcast_to(jnp.exp(glast), (nc, 8, GDN_HEAD_DIM)).reshape(nc * 8, GDN_HEAD_DIM)


def gdn_prepare(p_gdn, ba, conv_w, a_log, dt_bias, batch, seq, heads):
    gw = heads * GDN_HEAD_DIM
    tb = min(512, seq)
    nc = tb // GDN_CHUNK
    a_row = jnp.zeros((1, LANES), F32).at[0, heads:2 * heads].set(a_log)
    d_row = jnp.zeros((1, LANES), F32).at[0, heads:2 * heads].set(dt_bias)

    def col(j):
        return pl.BlockSpec((None, tb, gw), lambda b, t: (b, t, j))

    def cw(j):
        return pl.BlockSpec((GDN_CONV_WIDTH, gw), lambda b, t: (0, j))

    vec = pl.BlockSpec((1, LANES), lambda b, t: (0, 0))
    tok = pl.BlockSpec((None, tb, gw), lambda b, t: (b, t, 0))
    act = lambda dt: jax.ShapeDtypeStruct((batch, seq, gw), dt)
    kern = functools.partial(_gdn_prep_kernel, heads=heads, tb=tb)
    return pl.pallas_call(
        kern, grid=(batch, seq // tb),
        in_specs=[col(0), col(1), col(2),
                  pl.BlockSpec((None, tb, LANES), lambda b, t: (b, t, 0)),
                  cw(0), cw(1), cw(2), vec, vec],
        out_specs=[tok, tok, tok, tok, tok,
                   pl.BlockSpec((None, nc * 8, gw), lambda b, t: (b, t, 0))],
        out_shape=[act(F32), act(BF16), act(BF16), act(BF16), act(BF16),
                   jax.ShapeDtypeStruct((batch, seq // GDN_CHUNK * 8, gw), F32)],
        scratch_shapes=[pltpu.VMEM((tb + 8, gw), F32)] * 3,
        compiler_params=_params("parallel", "arbitrary"),
    )(p_gdn, p_gdn, p_gdn, ba, conv_w, conv_w, conv_w, a_row, d_row)


def _gdn_scan_kernel(u_ref, w_ref, qd_ref, kt_ref, qk_ref, dec_ref, z_ref, g_ref, o_ref, s_ref,
                     *, batch, heads, tb):
    c = GDN_CHUNK

    @pl.when(pl.program_id(0) == 0)
    def _():
        s_ref[...] = jnp.zeros_like(s_ref)

    def chunk(n, carry):
        r0 = pl.multiple_of(n * c, c)
        rows = pl.ds(r0, c)
        for b in range(batch):
            for h in range(heads):
                sl = slice(h * GDN_HEAD_DIM, (h + 1) * GDN_HEAD_DIM)
                s = s_ref[b * heads + h]
                sb = s.astype(BF16)
                v_new = u_ref[b, rows, sl] - _dot(w_ref[b, rows, sl], sb)
                vb = v_new.astype(BF16)
                qk = qk_ref[b, rows, sl][:, :c]
                o = _dot(qd_ref[b, rows, sl], sb) + _dot(qk, vb)
                ktv = lax.dot_general(kt_ref[b, rows, sl], vb, (((0,), (0,)), ((), ())),
                                      preferred_element_type=F32)
                dec = dec_ref[b, pl.ds(pl.multiple_of(n * 8, 8), 8), sl][0:1, :]
                s_ref[b * heads + h] = s * dec + ktv
                on = o * lax.rsqrt(jnp.mean(o * o, axis=-1, keepdims=True) + RMS_EPS) * g_ref[...]
                z = z_ref[b, rows, sl].astype(F32)
                o_ref[b, rows, sl] = (on * _silu(z)).astype(o_ref.dtype)
        return carry

    lax.fori_loop(0, tb // c, chunk, 0)


def gdn_scan(u, w, qd, kt, qk, dec, p_gdn, norm_g, batch, seq, heads):
    gw = heads * GDN_HEAD_DIM
    tb = min(512, seq)
    nc = tb // GDN_CHUNK
    tok = pl.BlockSpec((batch, tb, gw), lambda t: (0, t, 0))
    kern = functools.partial(_gdn_scan_kernel, batch=batch, heads=heads, tb=tb)
    return pl.pallas_call(
        kern, grid=(seq // tb,),
        in_specs=[tok, tok, tok, tok, tok,
                  pl.BlockSpec((batch, nc * 8, gw), lambda t: (0, t, 0)),
                  pl.BlockSpec((batch, tb, gw), lambda t: (0, t, 3)),
                  pl.BlockSpec((1, GDN_HEAD_DIM), lambda t: (0, 0))],
        out_specs=tok,
        out_shape=jax.ShapeDtypeStruct((batch, seq, gw), BF16),
        scratch_shapes=[pltpu.VMEM((batch * heads, GDN_HEAD_DIM, GDN_HEAD_DIM), F32)],
        compiler_params=_params("arbitrary"),
    )(u, w, qd, kt, qk, dec, p_gdn, norm_g.reshape(1, GDN_HEAD_DIM))


def _conf_kernel(u_ref, w_ref, b_ref, g_ref, beta_ref, o_ref, xe_ref, *, tb, ch, rb):
    halo = 32

    @pl.when(pl.program_id(1) == 0)
    def _():
        xe_ref[0:halo, :] = jnp.zeros((halo, ch), F32)

    u = u_ref[...]
    xe_ref[halo:halo + tb, :] = u[:, :ch].astype(F32) * _sigmoid(u[:, ch:].astype(F32))
    first = halo - (CONF_KERNEL - 1)

    for r in range(tb // rb):
        r0 = r * rb
        acc = jnp.broadcast_to(b_ref[...], (rb, ch))
        for j in range(CONF_KERNEL):
            acc = acc + w_ref[j:j + 1, :] * xe_ref[pl.ds(r0 + first + j, rb), :]
        mu = jnp.mean(acc, axis=-1, keepdims=True)
        xc = acc - mu
        var = jnp.mean(xc * xc, axis=-1, keepdims=True)
        y = xc * lax.rsqrt(var + RMS_EPS) * g_ref[...] + beta_ref[...]
        o_ref[pl.ds(r0, rb), :] = _silu(y).astype(o_ref.dtype)
    xe_ref[0:halo, :] = xe_ref[tb:tb + halo, :]


def conformer_conv(p_conf, dw_w, dw_b, ln_g, ln_b, batch, seq):
    ch = p_conf.shape[-1] // 2
    tb = min(512, seq)
    kern = functools.partial(_conf_kernel, tb=tb, ch=ch, rb=32)
    w_pad = jnp.zeros((32, ch), F32).at[:CONF_KERNEL].set(dw_w)
    vec = pl.BlockSpec((1, ch), lambda b, t: (0, 0))
    return pl.pallas_call(
        kern, grid=(batch, seq // tb),
        in_specs=[pl.BlockSpec((None, tb, 2 * ch), lambda b, t: (b, t, 0)),
                  pl.BlockSpec((32, ch), lambda b, t: (0, 0)), vec, vec, vec],
        out_specs=pl.BlockSpec((None, tb, ch), lambda b, t: (b, t, 0)),
        out_shape=jax.ShapeDtypeStruct((batch, seq, ch), BF16),
        scratch_shapes=[pltpu.VMEM((tb + 32, ch), F32)],
        compiler_params=_params("parallel", "arbitrary"),
    )(p_conf, w_pad, dw_b.reshape(1, ch), ln_g.reshape(1, ch), ln_b.reshape(1, ch))


def _diff_prep_kernel(q_ref, k_ref, ones_ref, gq_ref, gk_ref, qo_ref, ko_ref):
    def norm(x_ref, g_ref, o_ref):
        x = x_ref[...].astype(F32)
        ms = _dot((x * x).astype(BF16), ones_ref[...]) * (1.0 / DIFF_QK_DIM)
        o_ref[...] = (x * lax.rsqrt(ms + RMS_EPS) * g_ref[...]).astype(o_ref.dtype)

    norm(q_ref, gq_ref, qo_ref)
    norm(k_ref, gk_ref, ko_ref)


def diff_prepare(p_diff, q_gain, k_gain, heads):
    n = p_diff.shape[0]
    dw = heads * 2 * DIFF_QK_DIM
    tm = min(512, n)
    grp = jnp.arange(dw) // DIFF_QK_DIM
    ones = (grp[:, None] == grp[None, :]).astype(BF16)
    gq = (jnp.tile(q_gain, 2 * heads) * (DIFF_QK_DIM ** -0.5)).reshape(1, dw)
    gk = jnp.tile(k_gain, 2 * heads).reshape(1, dw)
    vec = pl.BlockSpec((1, dw), lambda i: (0, 0))
    out = pl.BlockSpec((tm, dw), lambda i: (i, 0))
    return pl.pallas_call(
        _diff_prep_kernel, grid=(n // tm,),
        in_specs=[pl.BlockSpec((tm, dw), lambda i: (i, 0)),
                  pl.BlockSpec((tm, dw), lambda i: (i, 1)),
                  pl.BlockSpec((dw, dw), lambda i: (0, 0)), vec, vec],
        out_specs=[out, out],
        out_shape=[jax.ShapeDtypeStruct((n, dw), BF16)] * 2,
        compiler_params=_params("parallel"),
    )(p_diff, p_diff, ones, gq, gk)


def _diff_attn_kernel(lam_ref, q_ref, k_ref, v_ref, sg_ref, o_ref, *, tq, heads, lam_init):
    h = pl.program_id(1)
    i = pl.program_id(2)
    d = DIFF_QK_DIM
    slope = jnp.exp2(jnp.full((1, 1), h + 1, jnp.int32).astype(F32) * (-8.0 / heads))
    lv = lam_ref[...]
    lam = (jnp.exp(jnp.sum(lv[0:1, :] * lv[1:2, :], axis=-1, keepdims=True))
           - jnp.exp(jnp.sum(lv[2:3, :] * lv[3:4, :], axis=-1, keepdims=True)) + lam_init)

    q = q_ref[...]
    lane = lax.broadcasted_iota(jnp.int32, q.shape, 1)
    zero = jnp.zeros_like(q)
    q2 = jnp.concatenate([jnp.where(lane < d, q, zero), jnp.where(lane >= d, q, zero)], axis=0)
    rr = lax.broadcasted_iota(jnp.int32, (tq, tq), 0)
    cc = lax.broadcasted_iota(jnp.int32, (tq, tq), 1)
    local = (cc - rr).astype(F32) * slope
    local2 = jnp.concatenate([local, local], axis=0)
    causal2 = jnp.concatenate([rr >= cc, rr >= cc], axis=0)

    def step(j, carry, masked):
        m, l, acc = carry
        start = pl.multiple_of(j * tq, tq)
        kj = k_ref[pl.ds(start, tq), :]
        vj = v_ref[pl.ds(start, tq), :]
        s = _dot_nt(q2, kj) + local2
        if masked:
            s = jnp.where(causal2, s, -jnp.inf)
        off = slope * ((j - i) * tq).astype(F32)
        m_new = jnp.maximum(m, jnp.max(s, axis=-1, keepdims=True) + off)
        p = jnp.exp(s - (m_new - off))
        alpha = jnp.exp(m - m_new)
        l = alpha * l + jnp.sum(p, axis=-1, keepdims=True)
        acc = alpha * acc + _dot(p.astype(BF16), vj)
        return m_new, l, acc

    init = (jnp.full((2 * tq, 1), -jnp.inf, F32), jnp.zeros((2 * tq, 1), F32),
            jnp.zeros((2 * tq, DIFF_V_DIM), F32))
    carry = lax.fori_loop(0, i, lambda j, cr: step(j, cr, False), init)
    _, l, acc = step(i, carry, True)
    o = acc / l
    out = o[:tq] - lam * o[tq:]
    out = out * lax.rsqrt(jnp.mean(out * out, axis=-1, keepdims=True) + RMS_EPS) * sg_ref[...]
    o_ref[...] = (out * (1.0 - lam_init)).astype(o_ref.dtype)


def diff_attention(qn, kn, p_diff, lam_vecs, sub_g, batch, seq, heads, lam_init):
    dw = heads * DIFF_V_DIM
    tq = min(256, seq)
    lam_pad = jnp.zeros((8, LANES), F32).at[:4, :DIFF_QK_DIM].set(lam_vecs)
    kern = functools.partial(_diff_attn_kernel, tq=tq, heads=heads, lam_init=lam_init)
    return pl.pallas_call(
        kern, grid=(batch, heads, seq // tq),
        in_specs=[pl.BlockSpec((8, LANES), lambda b, h, i: (0, 0)),
                  pl.BlockSpec((None, tq, LANES), lambda b, h, i: (b, i, h)),
                  pl.BlockSpec((None, seq, LANES), lambda b, h, i: (b, 0, h)),
                  pl.BlockSpec((None, seq, LANES), lambda b, h, i: (b, 0, 2 * heads + h)),
                  pl.BlockSpec((1, DIFF_V_DIM), lambda b, h, i: (0, 0))],
        out_specs=pl.BlockSpec((None, tq, LANES), lambda b, h, i: (b, i, h)),
        out_shape=jax.ShapeDtypeStruct((batch, seq, dw), BF16),
        compiler_params=_params("parallel", "parallel", "arbitrary"),
    )(lam_pad, qn, kn, p_diff, sub_g.reshape(1, DIFF_V_DIM))


def _merge_kernel(oa_ref, ob_ref, oc_ref, wa_ref, wb_ref, wc_ref, ga_ref, gb_ref, gc_ref, o_ref):
    mixed = _sigmoid(ga_ref[...].astype(F32)) * _dot(oa_ref[...], wa_ref[...])
    mixed = mixed + _sigmoid(gb_ref[...].astype(F32)) * _dot(ob_ref[...], wb_ref[...])
    mixed = mixed + _sigmoid(gc_ref[...].astype(F32)) * _dot(oc_ref[...], wc_ref[...])
    o_ref[...] = mixed.astype(o_ref.dtype)


def merge_branches(oa, ob, oc, wa, wb, wc, gates, d):
    n = oa.shape[0]
    tm = min(512, n)
    tn = min(512, d)
    nj = d // tn

    def lhs(x):
        return pl.BlockSpec((tm, x.shape[1]), lambda i, j: (i, 0))

    def rhs(w):
        return pl.BlockSpec((w.shape[0], tn), lambda i, j: (0, j))

    def gate(k):
        return pl.BlockSpec((tm, tn), lambda i, j: (i, k * nj + j))

    return pl.pallas_call(
        _merge_kernel, grid=(n // tm, nj),
        in_specs=[lhs(oa), lhs(ob), lhs(oc), rhs(wa), rhs(wb), rhs(wc), gate(0), gate(1), gate(2)],
        out_specs=pl.BlockSpec((tm, tn), lambda i, j: (i, j)),
        out_shape=jax.ShapeDtypeStruct((n, d), BF16),
        compiler_params=_params("parallel", "parallel"),
    )(oa, ob, oc, wa, wb, wc, gates, gates, gates)


def _proj_resid_kernel(a_ref, w_ref, x_ref, gate_ref, o_ref):
    o_ref[...] = x_ref[...] + gate_ref[...] * _dot(a_ref[...], w_ref[...])


def project_residual(a, w, x, mod4, gate_chunk, seq):
    n, k = a.shape
    d = w.shape[1]
    tm = min(1024, seq)
    tn = min(512, d)
    tpb = seq // tm
    return pl.pallas_call(
        _proj_resid_kernel, grid=(n // tm, d // tn),
        in_specs=[pl.BlockSpec((tm, k), lambda i, j: (i, 0)),
                  pl.BlockSpec((k, tn), lambda i, j: (0, j)),
                  pl.BlockSpec((tm, tn), lambda i, j: (i, j)),
                  pl.BlockSpec((None, None, 1, tn), lambda i, j: (i // tpb, gate_chunk, 0, j))],
        out_specs=pl.BlockSpec((tm, tn), lambda i, j: (i, j)),
        out_shape=jax.ShapeDtypeStruct((n, d), F32),
        compiler_params=_params("parallel", "parallel"),
    )(a, w, x, mod4)


def _route_rows(sig, sel):
    per_group = N_EXPERTS // N_GROUPS
    neg = jnp.full_like(sel[0], -jnp.inf)
    group_scores = []
    for g in range(N_GROUPS):
        r = sel[g * per_group:(g + 1) * per_group]
        best = None
        for a in range(per_group):
            for b in range(a + 1, per_group):
                pair = r[a] + r[b]
                best = pair if best is None else jnp.maximum(best, pair)
        group_scores.append(best)
    best_g = jnp.zeros_like(sel[0], dtype=jnp.int32)
    best_s = group_scores[0]
    for g in range(1, N_GROUPS):
        upd = group_scores[g] > best_s
        best_g = jnp.where(upd, g, best_g)
        best_s = jnp.where(upd, group_scores[g], best_s)
    masked = [jnp.where(best_g == (e // per_group), sel[e], neg) for e in range(N_EXPERTS)]

    def argmax_rows(rows):
        idx = jnp.zeros_like(best_g)
        val = rows[0]
        for e in range(1, N_EXPERTS):
            upd = rows[e] > val
            idx = jnp.where(upd, e, idx)
            val = jnp.where(upd, rows[e], val)
        return idx

    idx1 = argmax_rows(masked)
    idx2 = argmax_rows([jnp.where(idx1 == e, neg, masked[e]) for e in range(N_EXPERTS)])
    zero = jnp.zeros_like(sel[0])
    w1 = zero
    w2 = zero
    for e in range(N_EXPERTS):
        w1 = w1 + jnp.where(idx1 == e, sig[e], zero)
        w2 = w2 + jnp.where(idx2 == e, sig[e], zero)
    inv = 1.0 / (w1 + w2)
    return [(jnp.where(idx1 == e, w1, zero) + jnp.where(idx2 == e, w2, zero)) * inv
            for e in range(N_EXPERTS)]


def _norm_route_kernel(x_ref, g_ref, sc_ref, sh_ref, rw_ref, rb_ref, h_ref, comb_ref):
    h = _norm_mod(x_ref[...], g_ref[...], sc_ref[...], sh_ref[...])
    h_ref[...] = h.astype(h_ref.dtype)
    logits = lax.dot_general(rw_ref[...], h, (((1,), (1,)), ((), ())), precision=HIGHEST,
                             preferred_element_type=F32)
    sig_all = _sigmoid(logits)
    sel_all = sig_all + rb_ref[...]
    sig = [sig_all[e:e + 1, :] for e in range(N_EXPERTS)]
    sel = [sel_all[e:e + 1, :] for e in range(N_EXPERTS)]
    comb_ref[...] = jnp.concatenate(_route_rows(sig, sel), axis=0)


def norm_route(x, g, mod4, scale_chunk, shift_chunk, router_w, router_bias, seq):
    n, d = x.shape
    tm = min(256, seq)
    tpb = seq // tm
    row = pl.BlockSpec((tm, d), lambda i: (i, 0))
    vec = pl.BlockSpec((1, d), lambda i: (0, 0))
    return pl.pallas_call(
        _norm_route_kernel, grid=(n // tm,),
        in_specs=[row, vec, _mod_spec(d, tpb, scale_chunk), _mod_spec(d, tpb, shift_chunk),
                  pl.BlockSpec((N_EXPERTS, d), lambda i: (0, 0)),
                  pl.BlockSpec((N_EXPERTS, 1), lambda i: (0, 0))],
        out_specs=[row, pl.BlockSpec((N_EXPERTS, tm), lambda i: (0, i))],
        out_shape=[jax.ShapeDtypeStruct((n, d), BF16), jax.ShapeDtypeStruct((N_EXPERTS, n), F32)],
        compiler_params=_params("parallel"),
    )(x, g.reshape(1, d), mod4, mod4, router_w.T, router_bias.reshape(N_EXPERTS, 1))


def _moe_dense_kernel(x_ref, wg_ref, wu_ref, wd_ref, comb_ref, o_ref, acc_ref, *, n_f):
    e = pl.program_id(1)
    f = pl.program_id(2)

    @pl.when((e == 0) & (f == 0))
    def _():
        acc_ref[...] = jnp.zeros_like(acc_ref)

    x = x_ref[...]
    hid = _silu(_dot(x, wg_ref[...])) * _dot(x, wu_ref[...])
    comb = comb_ref[...]
    lane = lax.broadcasted_iota(jnp.int32, comb.shape, 1)
    ce = jnp.sum(jnp.where(lane == e, comb, 0.0), axis=-1, keepdims=True)
    acc_ref[...] += _dot((hid * ce).astype(BF16), wd_ref[...])

    @pl.when((e == N_EXPERTS - 1) & (f == n_f - 1))
    def _():
        o_ref[...] = acc_ref[...].astype(o_ref.dtype)


def moe_dense(h2, comb, wg, wu, wd):
    n, d = h2.shape
    ff = wg.shape[-1]
    tm = min(512, n)
    tf = min(256, ff)
    n_f = ff // tf
    kern = functools.partial(_moe_dense_kernel, n_f=n_f)
    return pl.pallas_call(
        kern, grid=(n // tm, N_EXPERTS, n_f),
        in_specs=[pl.BlockSpec((tm, d), lambda i, e, f: (i, 0)),
                  pl.BlockSpec((None, d, tf), lambda i, e, f: (e, 0, f)),
                  pl.BlockSpec((None, d, tf), lambda i, e, f: (e, 0, f)),
                  pl.BlockSpec((None, tf, d), lambda i, e, f: (e, f, 0)),
                  pl.BlockSpec((tm, LANES), lambda i, e, f: (i, 0))],
        out_specs=pl.BlockSpec((tm, d), lambda i, e, f: (i, 0)),
        out_shape=jax.ShapeDtypeStruct((n, d), BF16),
        scratch_shapes=[pltpu.VMEM((tm, d), F32)],
        compiler_params=_params("parallel", "arbitrary", "arbitrary"),
    )(h2, wg, wu, wd, comb)


def kernel(x, c, ada_w, ada_b, norm1_g, w_in, gdn_conv_w, gdn_a_log, gdn_dt_bias, gdn_norm_g, gdn_w_out, conf_dw_w, conf_dw_b, conf_ln_g, conf_ln_b, conf_w_out, diff_q_norm_g, diff_k_norm_g, diff_lambda_q1, diff_lambda_k1, diff_lambda_q2, diff_lambda_k2, diff_sub_g, diff_w_out, w_o, norm2_g, router_w, router_bias, exp_w_gate, exp_w_up, exp_w_down):
    batch, seq, d = x.shape
    n = batch * seq
    depth = ada_w.shape[0]
    heads = d // D_PER_HEAD
    gw = heads * GDN_HEAD_DIM
    dw = heads * DIFF_V_DIM
    conf_ch = conf_dw_w.shape[-1]
    o_ba = 4 * gw
    o_conf = o_ba + 2 * heads
    o_diff = o_conf + 2 * conf_ch
    o_gate = o_diff + 3 * dw

    mod = ada_modulation(c, ada_w, ada_b)
    xf = x.reshape(n, d)
    y_prev = None
    mod4_prev = None
    for l in range(depth):
        mod4 = mod[l].reshape(batch, ADA_CHUNKS, 1, d)
        if y_prev is None:
            h = norm_modulate(xf, norm1_g[l], mod4, 1, 0, seq)
        else:
            xf, h = norm_modulate(xf, norm1_g[l], mod4, 1, 0, seq,
                                  y=y_prev, gate_mod4=mod4_prev, gate_chunk=5)
        wl = w_in[l]
        w_ba = jnp.zeros((d, LANES), BF16).at[:, :2 * heads].set(wl[:, o_ba:o_conf].astype(BF16))
        p_gdn = matmul(h, wl[:, :o_ba].astype(BF16), BF16).reshape(batch, seq, 4 * gw)
        ba = matmul(h, w_ba, F32).reshape(batch, seq, LANES)
        p_conf = matmul(h, wl[:, o_conf:o_diff].astype(BF16), BF16).reshape(batch, seq, 2 * conf_ch)
        p_diff = matmul(h, wl[:, o_diff:o_gate].astype(BF16), BF16)
        gates = matmul(h, wl[:, o_gate:].astype(BF16), BF16)

        u, w, qd, kt, qk, dec = gdn_prepare(p_gdn, ba, gdn_conv_w[l], gdn_a_log[l], gdn_dt_bias[l],
                                            batch, seq, heads)
        o_a = gdn_scan(u, w, qd, kt, qk, dec, p_gdn, gdn_norm_g[l], batch, seq, heads)
        o_b = conformer_conv(p_conf, conf_dw_w[l], conf_dw_b[l], conf_ln_g[l], conf_ln_b[l], batch, seq)
        qn, kn = diff_prepare(p_diff, diff_q_norm_g[l], diff_k_norm_g[l], heads)
        lam_init = 0.8 - 0.6 * math.exp(-0.3 * l)
        lam_vecs = jnp.stack([diff_lambda_q1[l], diff_lambda_k1[l], diff_lambda_q2[l], diff_lambda_k2[l]])
        o_c = diff_attention(qn.reshape(batch, seq, dw), kn.reshape(batch, seq, dw),
                             p_diff.reshape(batch, seq, 3 * dw), lam_vecs, diff_sub_g[l],
                             batch, seq, heads, lam_init)
        mixed = merge_branches(o_a.reshape(n, gw), o_b.reshape(n, conf_ch), o_c.reshape(n, dw),
                               gdn_w_out[l].astype(BF16), conf_w_out[l].astype(BF16),
                               diff_w_out[l].astype(BF16), gates, d)
        xf = project_residual(mixed, w_o[l].astype(BF16), xf, mod4, 2, seq)

        h2, comb_t = norm_route(xf, norm2_g[l], mod4, 4, 3, router_w, router_bias, seq)
        comb = jnp.zeros((n, LANES), F32).at[:, :N_EXPERTS].set(comb_t.T)
        y_prev = moe_dense(h2, comb, exp_w_gate[l].astype(BF16), exp_w_up[l].astype(BF16),
                           exp_w_down[l].astype(BF16))
        mod4_prev = mod4
    xf = gated_residual(xf, y_prev, mod4_prev, 5, seq)
    return xf.reshape(batch, seq, d)
```

```python
import functools
import math

import jax
import jax.numpy as jnp
from jax import lax
from jax.experimental import pallas as pl
from jax.experimental.pallas import tpu as pltpu

F32 = jnp.float32
BF16 = jnp.bfloat16
HIGHEST = lax.Precision.HIGHEST
LOG2E = math.log2(math.e)

RMS_EPS = 1e-6
LANES = 128
GDN_HEAD_DIM = 128
GDN_CONV_WIDTH = 4
GDN_CHUNK = 64
CONF_KERNEL = 31
DIFF_QK_DIM = 64
DIFF_V_DIM = 128
N_EXPERTS = 16
N_GROUPS = 4
ADA_CHUNKS = 6
D_PER_HEAD = 512
VMEM_LIMIT_BYTES = 56 * 1024 * 1024


def _params(*semantics):
    return pltpu.CompilerParams(dimension_semantics=semantics,
                                vmem_limit_bytes=VMEM_LIMIT_BYTES)


def _sigmoid(x):
    return 1.0 / (1.0 + jnp.exp(-x))


def _silu(x):
    return x * _sigmoid(x)


def _softplus(x):
    return jnp.maximum(x, 0.0) + jnp.log(1.0 + jnp.exp(-jnp.abs(x)))


def _dot(a, b):
    return jnp.dot(a, b, preferred_element_type=F32)


def _dot_nt(a, b):
    return lax.dot_general(a, b, (((1,), (1,)), ((), ())), preferred_element_type=F32)


def _ada_kernel(c_ref, w_ref, b_ref, o_ref):
    cond = _silu(c_ref[...]).astype(BF16)
    o_ref[...] = _dot(cond, w_ref[...].astype(BF16)) + b_ref[...]


def ada_modulation(c, ada_w, ada_b):
    n_layers, d, d6 = ada_w.shape
    b = c.shape[0]
    rows = 8
    c_pad = jnp.zeros((rows, d), F32).at[:b].set(c)
    tn = min(512, d6)
    out = pl.pallas_call(
        _ada_kernel,
        grid=(n_layers, d6 // tn),
        in_specs=[pl.BlockSpec((rows, d), lambda l, j: (0, 0)),
                  pl.BlockSpec((None, d, tn), lambda l, j: (l, 0, j)),
                  pl.BlockSpec((None, 1, tn), lambda l, j: (l, 0, j))],
        out_specs=pl.BlockSpec((None, rows, tn), lambda l, j: (l, 0, j)),
        out_shape=jax.ShapeDtypeStruct((n_layers, rows, d6), F32),
        compiler_params=_params("parallel", "parallel"),
    )(c_pad, ada_w, ada_b.reshape(n_layers, 1, d6))
    return out[:, :b]


def _norm_mod(x, g, scale, shift):
    ms = jnp.mean(x * x, axis=-1, keepdims=True)
    y = x * lax.rsqrt(ms + RMS_EPS) * g
    return y * (1.0 + scale) + shift


def _normmod_kernel(x_ref, g_ref, sc_ref, sh_ref, h_ref):
    h_ref[...] = _norm_mod(x_ref[...], g_ref[...], sc_ref[...], sh_ref[...]).astype(h_ref.dtype)


def _resid_normmod_kernel(x_ref, y_ref, gate_ref, g_ref, sc_ref, sh_ref, xo_ref, h_ref):
    x = x_ref[...] + gate_ref[...] * y_ref[...].astype(F32)
    xo_ref[...] = x
    h_ref[...] = _norm_mod(x, g_ref[...], sc_ref[...], sh_ref[...]).astype(h_ref.dtype)


def _resid_kernel(x_ref, y_ref, gate_ref, xo_ref):
    xo_ref[...] = x_ref[...] + gate_ref[...] * y_ref[...].astype(F32)


def _mod_spec(d, tiles_per_batch, chunk):
    return pl.BlockSpec((None, None, 1, d), lambda i: (i // tiles_per_batch, chunk, 0, 0))


def norm_modulate(x, g, mod4, scale_chunk, shift_chunk, seq, y=None, gate_mod4=None, gate_chunk=None):
    n, d = x.shape
    tm = min(256, seq)
    tpb = seq // tm
    row = pl.BlockSpec((tm, d), lambda i: (i, 0))
    vec = pl.BlockSpec((1, d), lambda i: (0, 0))
    h_shape = jax.ShapeDtypeStruct((n, d), BF16)
    if y is None:
        return pl.pallas_call(
            _normmod_kernel, grid=(n // tm,),
            in_specs=[row, vec, _mod_spec(d, tpb, scale_chunk), _mod_spec(d, tpb, shift_chunk)],
            out_specs=row, out_shape=h_shape, compiler_params=_params("parallel"),
        )(x, g.reshape(1, d), mod4, mod4)
    return pl.pallas_call(
        _resid_normmod_kernel, grid=(n // tm,),
        in_specs=[row, row, _mod_spec(d, tpb, gate_chunk), vec,
                  _mod_spec(d, tpb, scale_chunk), _mod_spec(d, tpb, shift_chunk)],
        out_specs=[row, row],
        out_shape=[jax.ShapeDtypeStruct((n, d), F32), h_shape],
        compiler_params=_params("parallel"),
    )(x, y, gate_mod4, g.reshape(1, d), mod4, mod4)


def gated_residual(x, y, mod4, gate_chunk, seq):
    n, d = x.shape
    tm = min(256, seq)
    tpb = seq // tm
    row = pl.BlockSpec((tm, d), lambda i: (i, 0))
    return pl.pallas_call(
        _resid_kernel, grid=(n // tm,),
        in_specs=[row, row, _mod_spec(d, tpb, gate_chunk)],
        out_specs=row, out_shape=jax.ShapeDtypeStruct((n, d), F32),
        compiler_params=_params("parallel"),
    )(x, y, mod4)


def _mm_kernel(a_ref, b_ref, o_ref):
    o_ref[...] = _dot(a_ref[...], b_ref[...]).astype(o_ref.dtype)


def matmul(a, b, out_dtype, tm=1024, tn=512):
    m, k = a.shape
    n = b.shape[1]
    tm = min(tm, m)
    tn = min(tn, n)
    while n % tn:
        tn -= LANES
    return pl.pallas_call(
        _mm_kernel, grid=(m // tm, n // tn),
        in_specs=[pl.BlockSpec((tm, k), lambda i, j: (i, 0)),
                  pl.BlockSpec((k, tn), lambda i, j: (0, j))],
        out_specs=pl.BlockSpec((tm, tn), lambda i, j: (i, j)),
        out_shape=jax.ShapeDtypeStruct((m, n), out_dtype),
        compiler_params=_params("parallel", "parallel"),
    )(a, b)


def _unit_lower_inverse(lower, ii, jj):
    n = lower.shape[0]
    t = jnp.broadcast_to((ii == jj).astype(F32)[None], lower.shape)
    s = 1
    while s < GDN_CHUNK:
        sh = s.bit_length() - 1
        sel = ((ii >> (sh + 1)) == (jj >> (sh + 1))) & ((ii >> sh) != (jj >> sh)) & (ii > jj)
        lo = jnp.where(sel[None], lower, 0.0)
        tl = jnp.einsum('nij,njk->nik', t, lo, precision=HIGHEST, preferred_element_type=F32)
        t = t - jnp.einsum('nij,njk->nik', tl, t, precision=HIGHEST, preferred_element_type=F32)
        s *= 2
    del n
    return t


def _gdn_prep_kernel(q_ref, k_ref, v_ref, ba_ref, cwq_ref, cwk_ref, cwv_ref, alog_ref, dt_ref,
                     u_ref, w_ref, qd_ref, kt_ref, qk_ref, dec_ref,
                     xq_ref, xk_ref, xv_ref, *, heads, tb):
    c = GDN_CHUNK
    nc = tb // c
    halo = 8

    @pl.when(pl.program_id(1) == 0)
    def _():
        for xe in (xq_ref, xk_ref, xv_ref):
            xe[0:halo, :] = jnp.zeros((halo, xe.shape[1]), F32)

    def conv_silu(x_ref, xe, cw_ref):
        xe[halo:halo + tb, :] = x_ref[...].astype(F32)
        first = halo - (GDN_CONV_WIDTH - 1)
        acc = cw_ref[0:1, :] * xe[pl.ds(first, tb), :]
        for j in range(1, GDN_CONV_WIDTH):
            acc = acc + cw_ref[j:j + 1, :] * xe[pl.ds(first + j, tb), :]
        xe[0:halo, :] = xe[tb:tb + halo, :]
        return _silu(acc)

    qa = conv_silu(q_ref, xq_ref, cwq_ref)
    ka = conv_silu(k_ref, xk_ref, cwk_ref)
    va = conv_silu(v_ref, xv_ref, cwv_ref)

    ba = ba_ref[...]
    g_all = -jnp.exp(alog_ref[...]) * _softplus(ba + dt_ref[...])
    beta_all = _sigmoid(ba).reshape(nc, c, LANES)

    ii = lax.broadcasted_iota(jnp.int32, (c, c), 0)
    jj = lax.broadcasted_iota(jnp.int32, (c, c), 1)
    tril = ii >= jj
    strict = ii > jj
    eye = ii == jj
    tril_f = jnp.broadcast_to(tril.astype(F32)[None], (nc, c, c))
    ones_f = jnp.ones((nc, c, c), F32)
    gc_all = jnp.einsum('nij,njl->nil', tril_f, g_all.reshape(nc, c, LANES),
                        precision=HIGHEST, preferred_element_type=F32)

    for h in range(heads):
        sl = slice(h * GDN_HEAD_DIM, (h + 1) * GDN_HEAD_DIM)
        qh = qa[:, sl]
        kh = ka[:, sl]
        qh = qh * lax.rsqrt(jnp.sum(qh * qh, axis=-1, keepdims=True) + RMS_EPS) * (GDN_HEAD_DIM ** -0.5)
        kh = kh * lax.rsqrt(jnp.sum(kh * kh, axis=-1, keepdims=True) + RMS_EPS)
        q3 = qh.reshape(nc, c, GDN_HEAD_DIM)
        k3 = kh.reshape(nc, c, GDN_HEAD_DIM)
        v3 = va[:, sl].reshape(nc, c, GDN_HEAD_DIM)
        gcol = gc_all[:, :, heads + h:heads + h + 1]
        bcol = beta_all[:, :, h:h + 1]
        gcol_b = jnp.broadcast_to(gcol, (nc, c, c))
        grow_b = jnp.einsum('nim,nmj->nij', ones_f, jnp.where(eye[None], gcol_b, 0.0),
                            precision=HIGHEST, preferred_element_type=F32)
        decay = jnp.where(tril[None], jnp.exp(jnp.where(tril[None], gcol_b - grow_b, 0.0)), 0.0)
        glast = gcol[:, c - 1:c, :]
        eg = jnp.exp(gcol)
        kb = k3 * bcol
        k3b = k3.astype(BF16)
        kk = jnp.einsum('nid,njd->nij', kb.astype(BF16), k3b, preferred_element_type=F32)
        lower = jnp.where(strict[None], kk * decay, 0.0)
        t_inv = _unit_lower_inverse(lower, ii, jj).astype(BF16)
        u = jnp.einsum('nij,njd->nid', t_inv, (v3 * bcol).astype(BF16), preferred_element_type=F32)
        w = jnp.einsum('nij,njd->nid', t_inv, (kb * eg).astype(BF16), preferred_element_type=F32)
        qk = jnp.einsum('nid,njd->nij', q3.astype(BF16), k3b, preferred_element_type=F32) * decay
        u_ref[:, sl] = u.reshape(tb, GDN_HEAD_DIM)
        w_ref[:, sl] = w.reshape(tb, GDN_HEAD_DIM).astype(BF16)
        qd_ref[:, sl] = (q3 * eg).reshape(tb, GDN_HEAD_DIM).astype(BF16)
        kt_ref[:, sl] = (k3 * jnp.exp(glast - gcol)).reshape(tb, GDN_HEAD_DIM).astype(BF16)
        qk_pad = jnp.concatenate([qk, jnp.zeros_like(qk)], axis=-1)
        qk_ref[:, sl] = qk_pad.reshape(tb, GDN_HEAD_DIM).astype(BF16)
        dec_ref[:, sl] = jnp.broadcast_to(jnp.exp(glast), (nc, 8, GDN_HEAD_DIM)).reshape(nc * 8, GDN_HEAD_DIM)


def gdn_prepare(p_gdn, ba, conv_w, a_log, dt_bias, batch, seq, heads):
    gw = heads * GDN_HEAD_DIM
    tb = min(512, seq)
    nc = tb // GDN_CHUNK
    a_row = jnp.zeros((1, LANES), F32).at[0, heads:2 * heads].set(a_log)
    d_row = jnp.zeros((1, LANES), F32).at[0, heads:2 * heads].set(dt_bias)

    def col(j):
        return pl.BlockSpec((None, tb, gw), lambda b, t: (b, t, j))

    def cw(j):
        return pl.BlockSpec((GDN_CONV_WIDTH, gw), lambda b, t: (0, j))

    vec = pl.BlockSpec((1, LANES), lambda b, t: (0, 0))
    tok = pl.BlockSpec((None, tb, gw), lambda b, t: (b, t, 0))
    act = lambda dt: jax.ShapeDtypeStruct((batch, seq, gw), dt)
    kern = functools.partial(_gdn_prep_kernel, heads=heads, tb=tb)
    return pl.pallas_call(
        kern, grid=(batch, seq // tb),
        in_specs=[col(0), col(1), col(2),
                  pl.BlockSpec((None, tb, LANES), lambda b, t: (b, t, 0)),
                  cw(0), cw(1), cw(2), vec, vec],
        out_specs=[tok, tok, tok, tok, tok,
                   pl.BlockSpec((None, nc * 8, gw), lambda b, t: (b, t, 0))],
        out_shape=[act(F32), act(BF16), act(BF16), act(BF16), act(BF16),
                   jax.ShapeDtypeStruct((batch, seq // GDN_CHUNK * 8, gw), F32)],
        scratch_shapes=[pltpu.VMEM((tb + 8, gw), F32)] * 3,
        compiler_params=_params("parallel", "arbitrary"),
    )(p_gdn, p_gdn, p_gdn, ba, conv_w, conv_w, conv_w, a_row, d_row)


def _gdn_scan_kernel(u_ref, w_ref, qd_ref, kt_ref, qk_ref, dec_ref, z_ref, g_ref, o_ref, s_ref,
                     *, batch, heads, tb):
    c = GDN_CHUNK

    @pl.when(pl.program_id(0) == 0)
    def _():
        s_ref[...] = jnp.zeros_like(s_ref)

    def chunk(n, carry):
        r0 = pl.multiple_of(n * c, c)
        rows = pl.ds(r0, c)
        for b in range(batch):
            for h in range(heads):
                sl = slice(h * GDN_HEAD_DIM, (h + 1) * GDN_HEAD_DIM)
                s = s_ref[b * heads + h]
                sb = s.astype(BF16)
                v_new = u_ref[b, rows, sl] - _dot(w_ref[b, rows, sl], sb)
                vb = v_new.astype(BF16)
                qk = qk_ref[b, rows, sl][:, :c]
                o = _dot(qd_ref[b, rows, sl], sb) + _dot(qk, vb)
                ktv = lax.dot_general(kt_ref[b, rows, sl], vb, (((0,), (0,)), ((), ())),
                                      preferred_element_type=F32)
                dec = dec_ref[b, pl.ds(pl.multiple_of(n * 8, 8), 8), sl][0:1, :]
                s_ref[b * heads + h] = s * dec + ktv
                on = o * lax.rsqrt(jnp.mean(o * o, axis=-1, keepdims=True) + RMS_EPS) * g_ref[...]
                z = z_ref[b, rows, sl].astype(F32)
                o_ref[b, rows, sl] = (on * _silu(z)).astype(o_ref.dtype)
        return carry

    lax.fori_loop(0, tb // c, chunk, 0)


def gdn_scan(u, w, qd, kt, qk, dec, p_gdn, norm_g, batch, seq, heads):
    gw = heads * GDN_HEAD_DIM
    tb = min(512, seq)
    nc = tb // GDN_CHUNK
    tok = pl.BlockSpec((batch, tb, gw), lambda t: (0, t, 0))
    kern = functools.partial(_gdn_scan_kernel, batch=batch, heads=heads, tb=tb)
    return pl.pallas_call(
        kern, grid=(seq // tb,),
        in_specs=[tok, tok, tok, tok, tok,
                  pl.BlockSpec((batch, nc * 8, gw), lambda t: (0, t, 0)),
                  pl.BlockSpec((batch, tb, gw), lambda t: (0, t, 3)),
                  pl.BlockSpec((1, GDN_HEAD_DIM), lambda t: (0, 0))],
        out_specs=tok,
        out_shape=jax.ShapeDtypeStruct((batch, seq, gw), BF16),
        scratch_shapes=[pltpu.VMEM((batch * heads, GDN_HEAD_DIM, GDN_HEAD_DIM), F32)],
        compiler_params=_params("arbitrary"),
    )(u, w, qd, kt, qk, dec, p_gdn, norm_g.reshape(1, GDN_HEAD_DIM))


def _conf_kernel(u_ref, w_ref, b_ref, g_ref, beta_ref, o_ref, xe_ref, *, tb, ch, rb):
    halo = 32

    @pl.when(pl.program_id(1) == 0)
    def _():
        xe_ref[0:halo, :] = jnp.zeros((halo, ch), F32)

    u = u_ref[...]
    xe_ref[halo:halo + tb, :] = u[:, :ch].astype(F32) * _sigmoid(u[:, ch:].astype(F32))
    first = halo - (CONF_KERNEL - 1)

    for r in range(tb // rb):
        r0 = r * rb
        acc = jnp.broadcast_to(b_ref[...], (rb, ch))
        for j in range(CONF_KERNEL):
            acc = acc + w_ref[j:j + 1, :] * xe_ref[pl.ds(r0 + first + j, rb), :]
        mu = jnp.mean(acc, axis=-1, keepdims=True)
        xc = acc - mu
        var = jnp.mean(xc * xc, axis=-1, keepdims=True)
        y = xc * lax.rsqrt(var + RMS_EPS) * g_ref[...] + beta_ref[...]
        o_ref[pl.ds(r0, rb), :] = _silu(y).astype(o_ref.dtype)
    xe_ref[0:halo, :] = xe_ref[tb:tb + halo, :]


def conformer_conv(p_conf, dw_w, dw_b, ln_g, ln_b, batch, seq):
    ch = p_conf.shape[-1] // 2
    tb = min(512, seq)
    kern = functools.partial(_conf_kernel, tb=tb, ch=ch, rb=32)
    w_pad = jnp.zeros((32, ch), F32).at[:CONF_KERNEL].set(dw_w)
    vec = pl.BlockSpec((1, ch), lambda b, t: (0, 0))
    return pl.pallas_call(
        kern, grid=(batch, seq // tb),
        in_specs=[pl.BlockSpec((None, tb, 2 * ch), lambda b, t: (b, t, 0)),
                  pl.BlockSpec((32, ch), lambda b, t: (0, 0)), vec, vec, vec],
        out_specs=pl.BlockSpec((None, tb, ch), lambda b, t: (b, t, 0)),
        out_shape=jax.ShapeDtypeStruct((batch, seq, ch), BF16),
        scratch_shapes=[pltpu.VMEM((tb + 32, ch), F32)],
        compiler_params=_params("parallel", "arbitrary"),
    )(p_conf, w_pad, dw_b.reshape(1, ch), ln_g.reshape(1, ch), ln_b.reshape(1, ch))


def _diff_prep_kernel(q_ref, k_ref, ones_ref, gq_ref, gk_ref, qo_ref, ko_ref):
    def norm(x_ref, g_ref, o_ref):
        x = x_ref[...].astype(F32)
        ms = _dot((x * x).astype(BF16), ones_ref[...]) * (1.0 / DIFF_QK_DIM)
        o_ref[...] = (x * lax.rsqrt(ms + RMS_EPS) * g_ref[...]).astype(o_ref.dtype)

    norm(q_ref, gq_ref, qo_ref)
    norm(k_ref, gk_ref, ko_ref)


def diff_prepare(p_diff, q_gain, k_gain, heads):
    n = p_diff.shape[0]
    dw = heads * 2 * DIFF_QK_DIM
    tm = min(512, n)
    grp = jnp.arange(dw) // DIFF_QK_DIM
    ones = (grp[:, None] == grp[None, :]).astype(BF16)
    gq = (jnp.tile(q_gain, 2 * heads) * (DIFF_QK_DIM ** -0.5 * LOG2E)).reshape(1, dw)
    gk = jnp.tile(k_gain, 2 * heads).reshape(1, dw)
    vec = pl.BlockSpec((1, dw), lambda i: (0, 0))
    out = pl.BlockSpec((tm, dw), lambda i: (i, 0))
    return pl.pallas_call(
        _diff_prep_kernel, grid=(n // tm,),
        in_specs=[pl.BlockSpec((tm, dw), lambda i: (i, 0)),
                  pl.BlockSpec((tm, dw), lambda i: (i, 1)),
                  pl.BlockSpec((dw, dw), lambda i: (0, 0)), vec, vec],
        out_specs=[out, out],
        out_shape=[jax.ShapeDtypeStruct((n, dw), BF16)] * 2,
        compiler_params=_params("parallel"),
    )(p_diff, p_diff, ones, gq, gk)


ATTN_HEADS_PER_STEP = 2
ATTN_ROW_BLOCKS = 8


def _diff_attn_kernel(lam_ref, q_ref, k_ref, v_ref, sg_ref, o_ref, loc_ref, m_ref, acc_ref,
                      *, tq, heads, hps, lam_init):
    i = pl.program_id(2)
    d = DIFF_QK_DIM
    dv = DIFF_V_DIM
    lv = lam_ref[...]
    lam = (jnp.exp(jnp.sum(lv[0:1, :] * lv[1:2, :], axis=-1, keepdims=True))
           - jnp.exp(jnp.sum(lv[2:3, :] * lv[3:4, :], axis=-1, keepdims=True)) + lam_init)
    lane = lax.broadcasted_iota(jnp.int32, (tq, LANES), 1)
    rr = lax.broadcasted_iota(jnp.int32, (tq, tq), 0)
    cc = lax.broadcasted_iota(jnp.int32, (tq, tq), 1)
    ones_col = jnp.where(lane == 0, 1.0, 0.0).astype(BF16)
    m_ref[...] = jnp.full(m_ref.shape, -jnp.inf, F32)
    acc_ref[...] = jnp.zeros(acc_ref.shape, F32)

    slopes = []
    q2s = []
    for hh in range(hps):
        sl = slice(hh * LANES, (hh + 1) * LANES)
        h = pl.program_id(1) * hps + hh
        slope = jnp.exp2(jnp.full((1, 1), h + 1, jnp.int32).astype(F32) * (-8.0 / heads)) * LOG2E
        slopes.append(slope)
        q = q_ref[:, sl]
        zero = jnp.zeros_like(q)
        q2s.append(jnp.concatenate([jnp.where(lane < d, q, zero), jnp.where(lane >= d, q, zero)], axis=0))
        local = (cc - rr).astype(F32) * slope
        loc_ref[hh] = jnp.concatenate([local, local], axis=0)

    def step(j, masked):
        start = pl.multiple_of(j * tq, tq)
        rs = 2 * tq // ATTN_ROW_BLOCKS
        for hh in range(hps):
            sl = slice(hh * LANES, (hh + 1) * LANES)
            kj = k_ref[pl.ds(start, tq), sl]
            v2 = jnp.concatenate([v_ref[pl.ds(start, tq), sl], ones_col], axis=1)
            off = slopes[hh] * ((j - i) * tq).astype(F32)
            for r in range(ATTN_ROW_BLOCKS):
                rows = slice(r * rs, (r + 1) * rs)
                s = _dot_nt(q2s[hh][rows], kj) + loc_ref[hh, rows, :]
                if masked:
                    causal = rr >= cc
                    s = jnp.where(jnp.concatenate([causal, causal], axis=0)[rows], s, -jnp.inf)
                m_old = m_ref[hh, rows, :]
                m_new = jnp.maximum(m_old, jnp.max(s, axis=-1, keepdims=True) + off)
                mm = m_new - off
                p = jnp.concatenate([jnp.exp2(s[:, c * LANES:(c + 1) * LANES] - mm)
                                     for c in range(tq // LANES)], axis=1).astype(BF16)
                alpha = jnp.exp2(m_old - m_new)
                acc_old = acc_ref[hh, rows, :]
                acc_ref[hh, rows, :] = (jnp.concatenate([acc_old[:, :dv] * alpha, acc_old[:, dv:] * alpha], axis=1)
                                        + _dot(p, v2))
                m_ref[hh, rows, :] = m_new

    def body(j, carry):
        step(j, False)
        return carry

    lax.fori_loop(0, i, body, 0)
    step(i, True)
    for hh in range(hps):
        acc = acc_ref[hh]
        o = acc[:, :dv] / acc[:, dv:dv + 1]
        out = o[:tq] - lam * o[tq:]
        out = out * lax.rsqrt(jnp.mean(out * out, axis=-1, keepdims=True) + RMS_EPS) * sg_ref[...]
        o_ref[:, hh * LANES:(hh + 1) * LANES] = (out * (1.0 - lam_init)).astype(o_ref.dtype)


def diff_attention(qn, kn, p_diff, lam_vecs, sub_g, batch, seq, heads, lam_init):
    dw = heads * DIFF_V_DIM
    tq = min(512, seq)
    hps = ATTN_HEADS_PER_STEP
    wb = hps * LANES
    lam_pad = jnp.zeros((8, LANES), F32).at[:4, :DIFF_QK_DIM].set(lam_vecs)
    kern = functools.partial(_diff_attn_kernel, tq=tq, heads=heads, hps=hps, lam_init=lam_init)
    return pl.pallas_call(
        kern, grid=(batch, heads // hps, seq // tq),
        in_specs=[pl.BlockSpec((8, LANES), lambda b, h, i: (0, 0)),
                  pl.BlockSpec((None, tq, wb), lambda b, h, i: (b, i, h)),
                  pl.BlockSpec((None, seq, wb), lambda b, h, i: (b, 0, h)),
                  pl.BlockSpec((None, seq, wb), lambda b, h, i: (b, 0, 2 * (heads // hps) + h)),
                  pl.BlockSpec((1, DIFF_V_DIM), lambda b, h, i: (0, 0))],
        out_specs=pl.BlockSpec((None, tq, wb), lambda b, h, i: (b, i, h)),
        out_shape=jax.ShapeDtypeStruct((batch, seq, dw), BF16),
        scratch_shapes=[pltpu.VMEM((hps, 2 * tq, tq), F32), pltpu.VMEM((hps, 2 * tq, LANES), F32),
                        pltpu.VMEM((hps, 2 * tq, 2 * DIFF_V_DIM), F32)],
        compiler_params=_params("parallel", "parallel", "arbitrary"),
    )(lam_pad, qn, kn, p_diff, sub_g.reshape(1, DIFF_V_DIM))


def _merge_kernel(oa_ref, ob_ref, oc_ref, wa_ref, wb_ref, wc_ref, ga_ref, gb_ref, gc_ref, o_ref):
    mixed = _sigmoid(ga_ref[...].astype(F32)) * _dot(oa_ref[...], wa_ref[...])
    mixed = mixed + _sigmoid(gb_ref[...].astype(F32)) * _dot(ob_ref[...], wb_ref[...])
    mixed = mixed + _sigmoid(gc_ref[...].astype(F32)) * _dot(oc_ref[...], wc_ref[...])
    o_ref[...] = mixed.astype(o_ref.dtype)


def merge_branches(oa, ob, oc, wa, wb, wc, gates, d):
    n = oa.shape[0]
    tm = min(512, n)
    tn = min(512, d)
    nj = d // tn

    def lhs(x):
        return pl.BlockSpec((tm, x.shape[1]), lambda i, j: (i, 0))

    def rhs(w):
        return pl.BlockSpec((w.shape[0], tn), lambda i, j: (0, j))

    def gate(k):
        return pl.BlockSpec((tm, tn), lambda i, j: (i, k * nj + j))

    return pl.pallas_call(
        _merge_kernel, grid=(n // tm, nj),
        in_specs=[lhs(oa), lhs(ob), lhs(oc), rhs(wa), rhs(wb), rhs(wc), gate(0), gate(1), gate(2)],
        out_specs=pl.BlockSpec((tm, tn), lambda i, j: (i, j)),
        out_shape=jax.ShapeDtypeStruct((n, d), BF16),
        compiler_params=_params("parallel", "parallel"),
    )(oa, ob, oc, wa, wb, wc, gates, gates, gates)


def _proj_resid_kernel(a_ref, w_ref, x_ref, gate_ref, o_ref):
    o_ref[...] = x_ref[...] + gate_ref[...] * _dot(a_ref[...], w_ref[...])


def project_residual(a, w, x, mod4, gate_chunk, seq):
    n, k = a.shape
    d = w.shape[1]
    tm = min(1024, seq)
    tn = min(512, d)
    tpb = seq // tm
    return pl.pallas_call(
        _proj_resid_kernel, grid=(n // tm, d // tn),
        in_specs=[pl.BlockSpec((tm, k), lambda i, j: (i, 0)),
                  pl.BlockSpec((k, tn), lambda i, j: (0, j)),
                  pl.BlockSpec((tm, tn), lambda i, j: (i, j)),
                  pl.BlockSpec((None, None, 1, tn), lambda i, j: (i // tpb, gate_chunk, 0, j))],
        out_specs=pl.BlockSpec((tm, tn), lambda i, j: (i, j)),
        out_shape=jax.ShapeDtypeStruct((n, d), F32),
        compiler_params=_params("parallel", "parallel"),
    )(a, w, x, mod4)


def _route_rows(sig, sel):
    per_group = N_EXPERTS // N_GROUPS
    neg = jnp.full_like(sel[0], -jnp.inf)
    group_scores = []
    for g in range(N_GROUPS):
        r = sel[g * per_group:(g + 1) * per_group]
        best = None
        for a in range(per_group):
            for b in range(a + 1, per_group):
                pair = r[a] + r[b]
                best = pair if best is None else jnp.maximum(best, pair)
        group_scores.append(best)
    best_g = jnp.zeros_like(sel[0], dtype=jnp.int32)
    best_s = group_scores[0]
    for g in range(1, N_GROUPS):
        upd = group_scores[g] > best_s
        best_g = jnp.where(upd, g, best_g)
        best_s = jnp.where(upd, group_scores[g], best_s)
    masked = [jnp.where(best_g == (e // per_group), sel[e], neg) for e in range(N_EXPERTS)]

    def argmax_rows(rows):
        idx = jnp.zeros_like(best_g)
        val = rows[0]
        for e in range(1, N_EXPERTS):
            upd = rows[e] > val
            idx = jnp.where(upd, e, idx)
            val = jnp.where(upd, rows[e], val)
        return idx

    idx1 = argmax_rows(masked)
    idx2 = argmax_rows([jnp.where(idx1 == e, neg, masked[e]) for e in range(N_EXPERTS)])
    zero = jnp.zeros_like(sel[0])
    w1 = zero
    w2 = zero
    for e in range(N_EXPERTS):
        w1 = w1 + jnp.where(idx1 == e, sig[e], zero)
        w2 = w2 + jnp.where(idx2 == e, sig[e], zero)
    inv = 1.0 / (w1 + w2)
    return idx1, idx2, w1 * inv, w2 * inv


def _norm_route_kernel(x_ref, g_ref, sc_ref, sh_ref, rw_ref, rb_ref, tri_ref,
                       h_ref, ids_ref, wts_ref, cum_ref, carry_ref):
    @pl.when(pl.program_id(0) == 0)
    def _():
        carry_ref[...] = jnp.zeros_like(carry_ref)

    h = _norm_mod(x_ref[...], g_ref[...], sc_ref[...], sh_ref[...])
    h_ref[...] = h.astype(h_ref.dtype)
    logits = lax.dot_general(rw_ref[...], h, (((1,), (1,)), ((), ())), precision=HIGHEST,
                             preferred_element_type=F32)
    sig_all = _sigmoid(logits)
    sel_all = sig_all + rb_ref[...]
    sig = [sig_all[e:e + 1, :] for e in range(N_EXPERTS)]
    sel = [sel_all[e:e + 1, :] for e in range(N_EXPERTS)]
    idx1, idx2, w1, w2 = _route_rows(sig, sel)
    onehot = jnp.concatenate(
        [jnp.where((idx1 == e) | (idx2 == e), 1.0, 0.0) for e in range(N_EXPERTS)], axis=0)
    cum = _dot(onehot.astype(BF16), tri_ref[...]) + carry_ref[...]
    tm = cum.shape[1]
    carry_ref[...] = cum[:, tm - 1:tm]
    cum_ref[...] = cum
    zero = jnp.zeros_like(w1)
    rank1 = zero
    rank2 = zero
    for e in range(N_EXPERTS):
        rank1 = rank1 + jnp.where(idx1 == e, cum[e:e + 1, :], zero)
        rank2 = rank2 + jnp.where(idx2 == e, cum[e:e + 1, :], zero)
    izero = jnp.zeros_like(idx1)
    ids_ref[...] = jnp.concatenate([idx1, idx2, (rank1 - 1.0).astype(jnp.int32),
                                    (rank2 - 1.0).astype(jnp.int32)] + [izero] * 4, axis=0)
    wts_ref[...] = jnp.concatenate([w1, w2] + [zero] * 6, axis=0)


def norm_route(x, g, mod4, scale_chunk, shift_chunk, router_w, router_bias, seq):
    n, d = x.shape
    tm = min(256, seq)
    tpb = seq // tm
    row = pl.BlockSpec((tm, d), lambda i: (i, 0))
    vec = pl.BlockSpec((1, d), lambda i: (0, 0))
    tri = (jnp.arange(tm)[:, None] <= jnp.arange(tm)[None, :]).astype(BF16)
    info = pl.BlockSpec((8, tm), lambda i: (0, i))
    return pl.pallas_call(
        _norm_route_kernel, grid=(n // tm,),
        in_specs=[row, vec, _mod_spec(d, tpb, scale_chunk), _mod_spec(d, tpb, shift_chunk),
                  pl.BlockSpec((N_EXPERTS, d), lambda i: (0, 0)),
                  pl.BlockSpec((N_EXPERTS, 1), lambda i: (0, 0)),
                  pl.BlockSpec((tm, tm), lambda i: (0, 0))],
        out_specs=[row, info, info, pl.BlockSpec((N_EXPERTS, tm), lambda i: (0, i))],
        out_shape=[jax.ShapeDtypeStruct((n, d), BF16), jax.ShapeDtypeStruct((8, n), jnp.int32),
                   jax.ShapeDtypeStruct((8, n), F32), jax.ShapeDtypeStruct((N_EXPERTS, n), F32)],
        scratch_shapes=[pltpu.VMEM((N_EXPERTS, 1), F32)],
        compiler_params=_params("arbitrary"),
    )(x, g.reshape(1, d), mod4, mod4, router_w.T, router_bias.reshape(N_EXPERTS, 1), tri)


MOE_ROW_TILE = 256
MOE_FFN_TILE = 512
MOE_TOK_CHUNK = 512


def moe_plan(ids, cum, n):
    tr, tf, tc = MOE_ROW_TILE, MOE_FFN_TILE, MOE_TOK_CHUNK
    tc = min(tc, n)
    e1, e2, r1, r2 = ids[0], ids[1], ids[2], ids[3]
    cnt = cum[:, n - 1].astype(jnp.int32)
    n_rt = 2 * n // tr + N_EXPERTS * (tf // tr)
    n_ft = n_rt * tr // tf
    rt_per_e = ((cnt + tf - 1) // tf) * (tf // tr)
    rt_end = jnp.cumsum(rt_per_e)
    rt_start = rt_end - rt_per_e
    used_rt = rt_end[-1]
    row_start = rt_start * tr
    d1 = row_start[e1] + r1
    d2 = row_start[e2] + r2
    rt = jnp.arange(n_rt, dtype=jnp.int32)
    rt_exp = jnp.clip(jnp.searchsorted(rt_end, rt, side='right'), 0, N_EXPERTS - 1).astype(jnp.int32)
    ft = jnp.arange(n_ft, dtype=jnp.int32)
    ft_exp = rt_exp[ft * (tf // tr)]
    used_ft = used_rt // (tf // tr)
    chunk_cum = cum[:, tc - 1::tc].astype(jnp.int32)
    n_chunks = n // tc
    a = (rt - rt_start[rt_exp]) * tr
    b = jnp.minimum(a + tr, cnt[rt_exp])
    cc = chunk_cum[rt_exp]
    lo_c = jnp.sum(cc <= a[:, None], axis=1)
    hi_c = jnp.sum(cc < b[:, None], axis=1)
    lo_c = jnp.clip(lo_c, 0, n_chunks - 1)
    hi_c = jnp.clip(jnp.maximum(hi_c, lo_c), 0, n_chunks - 1)
    n_c = jnp.where(rt < used_rt, hi_c - lo_c + 1, 0)
    item_end = jnp.cumsum(n_c)
    item_start = item_end - n_c
    total = item_end[-1]
    n_items = n_rt + N_EXPERTS * n_chunks
    w = jnp.arange(n_items, dtype=jnp.int32)
    valid = w < total
    wt = jnp.clip(jnp.searchsorted(item_end, jnp.minimum(w, total - 1), side='right'), 0, n_rt - 1)
    wt = wt.astype(jnp.int32)
    wc = (lo_c[wt] + (jnp.minimum(w, total - 1) - item_start[wt])).astype(jnp.int32)
    g_first = (valid & (w == item_start[wt])).astype(jnp.int32)
    g_last = (valid & (w == item_end[wt] - 1)).astype(jnp.int32)
    gather_items = (wt, wc, g_first, g_last, valid.astype(jnp.int32))
    key = jnp.where(valid, wc * n_rt + wt, jnp.iinfo(jnp.int32).max)
    order = jnp.argsort(key)
    st = wt[order]
    sc = wc[order]
    sv = valid[order]
    last_valid = jnp.maximum(total - 1, 0)
    st = jnp.where(sv, st, st[last_valid])
    sc = jnp.where(sv, sc, sc[last_valid])
    prev_c = jnp.concatenate([jnp.full((1,), -1, jnp.int32), sc[:-1]])
    next_c = jnp.concatenate([sc[1:], jnp.full((1,), -1, jnp.int32)])
    next_v = jnp.concatenate([sv[1:], jnp.zeros((1,), bool)])
    s_first = (sv & (sc != prev_c)).astype(jnp.int32)
    s_last = (sv & ((sc != next_c) | ~next_v)).astype(jnp.int32)
    scatter_items = (st, sc, s_first, s_last, sv.astype(jnp.int32))
    return dict(d1=d1, d2=d2, n_rt=n_rt, n_ft=n_ft, ft_exp=ft_exp, used_ft=used_ft.astype(jnp.int32),
                gather_items=gather_items, scatter_items=scatter_items, n_items=n_items, tc=tc)


def _moe_gather_kernel(tile_ref, chunk_ref, first_ref, last_ref, valid_ref,
                       h_ref, dest_ref, wts_ref, xs_ref, ws_ref, acc_ref, wacc_ref):
    w = pl.program_id(0)
    tr = xs_ref.shape[0]

    @pl.when(first_ref[w] == 1)
    def _():
        acc_ref[...] = jnp.zeros_like(acc_ref)
        wacc_ref[...] = jnp.zeros_like(wacc_ref)

    @pl.when(valid_ref[w] == 1)
    def _():
        rows = tile_ref[w] * tr + lax.broadcasted_iota(jnp.int32, (tr, 1), 0)
        hit1 = dest_ref[0:1, :] == rows
        hit2 = dest_ref[1:2, :] == rows
        onehot = jnp.where(hit1 | hit2, 1.0, 0.0).astype(BF16)
        acc_ref[...] += _dot(onehot, h_ref[...])
        pw = jnp.where(hit1, wts_ref[0:1, :], 0.0) + jnp.where(hit2, wts_ref[1:2, :], 0.0)
        wacc_ref[...] += jnp.sum(pw, axis=1, keepdims=True)

    @pl.when(last_ref[w] == 1)
    def _():
        xs_ref[...] = acc_ref[...].astype(xs_ref.dtype)
        ws_ref[...] = wacc_ref[...]


def moe_gather(h2, dest, wts, plan):
    n, d = h2.shape
    tr, tc = MOE_ROW_TILE, plan['tc']
    n_rows = plan['n_rt'] * tr
    grid_spec = pltpu.PrefetchScalarGridSpec(
        num_scalar_prefetch=5, grid=(plan['n_items'],),
        in_specs=[pl.BlockSpec((tc, d), lambda w, t, c, f, l, v: (c[w], 0)),
                  pl.BlockSpec((2, tc), lambda w, t, c, f, l, v: (0, c[w])),
                  pl.BlockSpec((8, tc), lambda w, t, c, f, l, v: (0, c[w]))],
        out_specs=[pl.BlockSpec((tr, d), lambda w, t, c, f, l, v: (t[w], 0)),
                   pl.BlockSpec((tr, 1), lambda w, t, c, f, l, v: (t[w], 0))],
        scratch_shapes=[pltpu.VMEM((tr, d), F32), pltpu.VMEM((tr, 1), F32)])
    return pl.pallas_call(
        _moe_gather_kernel, grid_spec=grid_spec,
        out_shape=[jax.ShapeDtypeStruct((n_rows, d), BF16), jax.ShapeDtypeStruct((n_rows, 1), F32)],
        compiler_params=_params("arbitrary"),
    )(*plan['gather_items'], h2, dest, wts)


def _moe_ffn_kernel(exp_ref, used_ref, x_ref, wg_ref, wu_ref, wd_ref, ws_ref, o_ref, acc_ref, *, n_f):
    i = pl.program_id(0)
    f = pl.program_id(1)
    used = i < used_ref[0]

    @pl.when(f == 0)
    def _():
        acc_ref[...] = jnp.zeros_like(acc_ref)

    @pl.when(used)
    def _():
        x = x_ref[...]
        hid = _silu(_dot(x, wg_ref[...])) * _dot(x, wu_ref[...])
        acc_ref[...] += _dot((hid * ws_ref[...]).astype(BF16), wd_ref[...])

    @pl.when(f == n_f - 1)
    def _():
        o_ref[...] = acc_ref[...].astype(o_ref.dtype)


def moe_ffn(xs, ws, wg, wu, wd, plan):
    n_rows, d = xs.shape
    ff = wg.shape[-1]
    tm = MOE_FFN_TILE
    tf = min(512, ff)
    n_f = ff // tf
    kern = functools.partial(_moe_ffn_kernel, n_f=n_f)

    def row(i, f, e, u):
        return (jnp.minimum(i, u[0] - 1), 0)

    grid_spec = pltpu.PrefetchScalarGridSpec(
        num_scalar_prefetch=2, grid=(plan['n_ft'], n_f),
        in_specs=[pl.BlockSpec((tm, d), row),
                  pl.BlockSpec((None, d, tf), lambda i, f, e, u: (e[i], 0, f)),
                  pl.BlockSpec((None, d, tf), lambda i, f, e, u: (e[i], 0, f)),
                  pl.BlockSpec((None, tf, d), lambda i, f, e, u: (e[i], f, 0)),
                  pl.BlockSpec((tm, 1), row)],
        out_specs=pl.BlockSpec((tm, d), lambda i, f, e, u: (i, 0)),
        scratch_shapes=[pltpu.VMEM((tm, d), F32)])
    return pl.pallas_call(
        kern, grid_spec=grid_spec,
        out_shape=jax.ShapeDtypeStruct((n_rows, d), BF16),
        compiler_params=_params("arbitrary", "arbitrary"),
    )(plan['ft_exp'], plan['used_ft'].reshape(1), xs, wg, wu, wd, ws)


def _moe_scatter_kernel(tile_ref, chunk_ref, first_ref, last_ref, valid_ref,
                        ys_ref, dest_ref, y_ref, acc_ref):
    w = pl.program_id(0)
    tr = ys_ref.shape[0]

    @pl.when(first_ref[w] == 1)
    def _():
        acc_ref[...] = jnp.zeros_like(acc_ref)

    @pl.when(valid_ref[w] == 1)
    def _():
        rows = tile_ref[w] * tr + lax.broadcasted_iota(jnp.int32, (1, tr), 1)
        hit = (dest_ref[:, 0:1] == rows) | (dest_ref[:, 1:2] == rows)
        acc_ref[...] += _dot(jnp.where(hit, 1.0, 0.0).astype(BF16), ys_ref[...])

    @pl.when(last_ref[w] == 1)
    def _():
        y_ref[...] = acc_ref[...].astype(y_ref.dtype)


def moe_scatter(ys, dest_t, plan, n):
    d = ys.shape[1]
    tr, tc = MOE_ROW_TILE, plan['tc']
    grid_spec = pltpu.PrefetchScalarGridSpec(
        num_scalar_prefetch=5, grid=(plan['n_items'],),
        in_specs=[pl.BlockSpec((tr, d), lambda w, t, c, f, l, v: (t[w], 0)),
                  pl.BlockSpec((tc, 2), lambda w, t, c, f, l, v: (c[w], 0))],
        out_specs=pl.BlockSpec((tc, d), lambda w, t, c, f, l, v: (c[w], 0)),
        scratch_shapes=[pltpu.VMEM((tc, d), F32)])
    return pl.pallas_call(
        _moe_scatter_kernel, grid_spec=grid_spec,
        out_shape=jax.ShapeDtypeStruct((n, d), BF16),
        compiler_params=_params("arbitrary"),
    )(*plan['scatter_items'], ys, dest_t)


def moe_sparse(h2, ids, wts, cum, wg, wu, wd):
    n = h2.shape[0]
    plan = moe_plan(ids, cum, n)
    dest = jnp.stack([plan['d1'], plan['d2']])
    xs, ws = moe_gather(h2, dest, wts, plan)
    ys = moe_ffn(xs, ws, wg, wu, wd, plan)
    return moe_scatter(ys, dest.T, plan, n)


def kernel(x, c, ada_w, ada_b, norm1_g, w_in, gdn_conv_w, gdn_a_log, gdn_dt_bias, gdn_norm_g, gdn_w_out, conf_dw_w, conf_dw_b, conf_ln_g, conf_ln_b, conf_w_out, diff_q_norm_g, diff_k_norm_g, diff_lambda_q1, diff_lambda_k1, diff_lambda_q2, diff_lambda_k2, diff_sub_g, diff_w_out, w_o, norm2_g, router_w, router_bias, exp_w_gate, exp_w_up, exp_w_down):
    batch, seq, d = x.shape
    n = batch * seq
    depth = ada_w.shape[0]
    heads = d // D_PER_HEAD
    gw = heads * GDN_HEAD_DIM
    dw = heads * DIFF_V_DIM
    conf_ch = conf_dw_w.shape[-1]
    o_ba = 4 * gw
    o_conf = o_ba + 2 * heads
    o_diff = o_conf + 2 * conf_ch
    o_gate = o_diff + 3 * dw

    mod = ada_modulation(c, ada_w, ada_b)
    xf = x.reshape(n, d)
    y_prev = None
    mod4_prev = None
    for l in range(depth):
        mod4 = mod[l].reshape(batch, ADA_CHUNKS, 1, d)
        if y_prev is None:
            h = norm_modulate(xf, norm1_g[l], mod4, 1, 0, seq)
        else:
            xf, h = norm_modulate(xf, norm1_g[l], mod4, 1, 0, seq,
                                  y=y_prev, gate_mod4=mod4_prev, gate_chunk=5)
        wl = w_in[l]
        w_ba = jnp.zeros((d, LANES), BF16).at[:, :2 * heads].set(wl[:, o_ba:o_conf].astype(BF16))
        p_gdn = matmul(h, wl[:, :o_ba].astype(BF16), BF16).reshape(batch, seq, 4 * gw)
        ba = matmul(h, w_ba, F32).reshape(batch, seq, LANES)
        p_conf = matmul(h, wl[:, o_conf:o_diff].astype(BF16), BF16).reshape(batch, seq, 2 * conf_ch)
        p_diff = matmul(h, wl[:, o_diff:o_gate].astype(BF16), BF16)
        gates = matmul(h, wl[:, o_gate:].astype(BF16), BF16)

        u, w, qd, kt, qk, dec = gdn_prepare(p_gdn, ba, gdn_conv_w[l], gdn_a_log[l], gdn_dt_bias[l],
                                            batch, seq, heads)
        o_a = gdn_scan(u, w, qd, kt, qk, dec, p_gdn, gdn_norm_g[l], batch, seq, heads)
        o_b = conformer_conv(p_conf, conf_dw_w[l], conf_dw_b[l], conf_ln_g[l], conf_ln_b[l], batch, seq)
        qn, kn = diff_prepare(p_diff, diff_q_norm_g[l], diff_k_norm_g[l], heads)
        lam_init = 0.8 - 0.6 * math.exp(-0.3 * l)
        lam_vecs = jnp.stack([diff_lambda_q1[l], diff_lambda_k1[l], diff_lambda_q2[l], diff_lambda_k2[l]])
        o_c = diff_attention(qn.reshape(batch, seq, dw), kn.reshape(batch, seq, dw),
                             p_diff.reshape(batch, seq, 3 * dw), lam_vecs, diff_sub_g[l],
                             batch, seq, heads, lam_init)
        mixed = merge_branches(o_a.reshape(n, gw), o_b.reshape(n, conf_ch), o_c.reshape(n, dw),
                               gdn_w_out[l].astype(BF16), conf_w_out[l].astype(BF16),
                               diff_w_out[l].astype(BF16), gates, d)
        xf = project_residual(mixed, w_o[l].astype(BF16), xf, mod4, 2, seq)

        h2, ids, wts, cum = norm_route(xf, norm2_g[l], mod4, 4, 3, router_w, router_bias, seq)
        y_prev = moe_sparse(h2, ids, wts, cum, exp_w_gate[l].astype(BF16), exp_w_up[l].astype(BF16),
                            exp_w_down[l].astype(BF16))
        mod4_prev = mod4
    xf = gated_residual(xf, y_prev, mod4_prev, 5, seq)
    return xf.reshape(batch, seq, d)
```

```python
import functools
import math

import jax
import jax.numpy as jnp
from jax import lax
from jax.experimental import pallas as pl
from jax.experimental.pallas import tpu as pltpu

F32 = jnp.float32
BF16 = jnp.bfloat16
HIGHEST = lax.Precision.HIGHEST
LOG2E = math.log2(math.e)

RMS_EPS = 1e-6
LANES = 128
GDN_HEAD_DIM = 128
GDN_CONV_WIDTH = 4
GDN_CHUNK = 64
CONF_KERNEL = 31
DIFF_QK_DIM = 64
DIFF_V_DIM = 128
N_EXPERTS = 16
N_GROUPS = 4
ADA_CHUNKS = 6
D_PER_HEAD = 512
VMEM_LIMIT_BYTES = 56 * 1024 * 1024


def _params(*semantics):
    return pltpu.CompilerParams(dimension_semantics=semantics,
                                vmem_limit_bytes=VMEM_LIMIT_BYTES)


def _sigmoid(x):
    return 1.0 / (1.0 + jnp.exp(-x))


def _silu(x):
    return x * _sigmoid(x)


def _softplus(x):
    return jnp.maximum(x, 0.0) + jnp.log(1.0 + jnp.exp(-jnp.abs(x)))


def _dot(a, b):
    return jnp.dot(a, b, preferred_element_type=F32)


def _dot_nt(a, b):
    return lax.dot_general(a, b, (((1,), (1,)), ((), ())), preferred_element_type=F32)


def _ada_kernel(c_ref, w_ref, b_ref, o_ref):
    cond = _silu(c_ref[...]).astype(BF16)
    o_ref[...] = _dot(cond, w_ref[...].astype(BF16)) + b_ref[...]


def ada_modulation(c, ada_w, ada_b):
    n_layers, d, d6 = ada_w.shape
    b = c.shape[0]
    rows = 8
    c_pad = jnp.zeros((rows, d), F32).at[:b].set(c)
    tn = min(512, d6)
    out = pl.pallas_call(
        _ada_kernel,
        grid=(n_layers, d6 // tn),
        in_specs=[pl.BlockSpec((rows, d), lambda l, j: (0, 0)),
                  pl.BlockSpec((None, d, tn), lambda l, j: (l, 0, j)),
                  pl.BlockSpec((None, 1, tn), lambda l, j: (l, 0, j))],
        out_specs=pl.BlockSpec((None, rows, tn), lambda l, j: (l, 0, j)),
        out_shape=jax.ShapeDtypeStruct((n_layers, rows, d6), F32),
        compiler_params=_params("parallel", "parallel"),
    )(c_pad, ada_w, ada_b.reshape(n_layers, 1, d6))
    return out[:, :b]


def _norm_mod(x, g, scale, shift):
    ms = jnp.mean(x * x, axis=-1, keepdims=True)
    y = x * lax.rsqrt(ms + RMS_EPS) * g
    return y * (1.0 + scale) + shift


def _normmod_kernel(x_ref, g_ref, sc_ref, sh_ref, h_ref):
    h_ref[...] = _norm_mod(x_ref[...], g_ref[...], sc_ref[...], sh_ref[...]).astype(h_ref.dtype)


def _resid_normmod_kernel(x_ref, y_ref, gate_ref, g_ref, sc_ref, sh_ref, xo_ref, h_ref):
    x = x_ref[...] + gate_ref[...] * y_ref[...].astype(F32)
    xo_ref[...] = x
    h_ref[...] = _norm_mod(x, g_ref[...], sc_ref[...], sh_ref[...]).astype(h_ref.dtype)


def _resid_kernel(x_ref, y_ref, gate_ref, xo_ref):
    xo_ref[...] = x_ref[...] + gate_ref[...] * y_ref[...].astype(F32)


def _mod_spec(d, tiles_per_batch, chunk):
    return pl.BlockSpec((None, None, 1, d), lambda i: (i // tiles_per_batch, chunk, 0, 0))


def norm_modulate(x, g, mod4, scale_chunk, shift_chunk, seq, y=None, gate_mod4=None, gate_chunk=None):
    n, d = x.shape
    tm = min(256, seq)
    tpb = seq // tm
    row = pl.BlockSpec((tm, d), lambda i: (i, 0))
    vec = pl.BlockSpec((1, d), lambda i: (0, 0))
    h_shape = jax.ShapeDtypeStruct((n, d), BF16)
    if y is None:
        return pl.pallas_call(
            _normmod_kernel, grid=(n // tm,),
            in_specs=[row, vec, _mod_spec(d, tpb, scale_chunk), _mod_spec(d, tpb, shift_chunk)],
            out_specs=row, out_shape=h_shape, compiler_params=_params("parallel"),
        )(x, g.reshape(1, d), mod4, mod4)
    return pl.pallas_call(
        _resid_normmod_kernel, grid=(n // tm,),
        in_specs=[row, row, _mod_spec(d, tpb, gate_chunk), vec,
                  _mod_spec(d, tpb, scale_chunk), _mod_spec(d, tpb, shift_chunk)],
        out_specs=[row, row],
        out_shape=[jax.ShapeDtypeStruct((n, d), F32), h_shape],
        compiler_params=_params("parallel"),
    )(x, y, gate_mod4, g.reshape(1, d), mod4, mod4)


def gated_residual(x, y, mod4, gate_chunk, seq):
    n, d = x.shape
    tm = min(256, seq)
    tpb = seq // tm
    row = pl.BlockSpec((tm, d), lambda i: (i, 0))
    return pl.pallas_call(
        _resid_kernel, grid=(n // tm,),
        in_specs=[row, row, _mod_spec(d, tpb, gate_chunk)],
        out_specs=row, out_shape=jax.ShapeDtypeStruct((n, d), F32),
        compiler_params=_params("parallel"),
    )(x, y, mod4)


def _mm_kernel(a_ref, b_ref, o_ref):
    o_ref[...] = _dot(a_ref[...], b_ref[...]).astype(o_ref.dtype)


def matmul(a, b, out_dtype, tm=1024, tn=512):
    m, k = a.shape
    n = b.shape[1]
    tm = min(tm, m)
    tn = min(tn, n)
    while n % tn:
        tn -= LANES
    return pl.pallas_call(
        _mm_kernel, grid=(m // tm, n // tn),
        in_specs=[pl.BlockSpec((tm, k), lambda i, j: (i, 0)),
                  pl.BlockSpec((k, tn), lambda i, j: (0, j))],
        out_specs=pl.BlockSpec((tm, tn), lambda i, j: (i, j)),
        out_shape=jax.ShapeDtypeStruct((m, n), out_dtype),
        compiler_params=_params("parallel", "parallel"),
    )(a, b)


def _unit_lower_inverse(lower, ii, jj):
    t = jnp.broadcast_to((ii == jj).astype(F32)[None], lower.shape)
    s = 1
    while s < GDN_CHUNK:
        sh = s.bit_length() - 1
        sel = ((ii >> (sh + 1)) == (jj >> (sh + 1))) & ((ii >> sh) != (jj >> sh)) & (ii > jj)
        lo = jnp.where(sel[None], lower, 0.0)
        t_parts = _split_bf16(t, 2)
        tl = _bmm_split(t_parts, _split_bf16(lo, 2))
        t = t - _bmm_split(_split_bf16(tl, 2), t_parts)
        s *= 2
    return t


def _split_bf16(x, n):
    parts = []
    for _ in range(n - 1):
        p = x.astype(BF16)
        parts.append(p)
        x = x - p.astype(F32)
    parts.append(x.astype(BF16))
    return parts


def _bmm_split(a_parts, b_parts):
    out = None
    for ia, a in enumerate(a_parts):
        for ib, b in enumerate(b_parts):
            if ia + ib >= max(len(a_parts), len(b_parts)):
                continue
            term = jnp.einsum('nij,njk->nik', a, b, preferred_element_type=F32)
            out = term if out is None else out + term
    return out


def _gdn_prep_kernel(q_ref, k_ref, v_ref, ba_ref, cwq_ref, cwk_ref, cwv_ref, alog_ref, dt_ref,
                     u_ref, w_ref, qd_ref, kt_ref, qk_ref, dec_ref,
                     xq_ref, xk_ref, xv_ref, *, heads, tb):
    c = GDN_CHUNK
    nc = tb // c
    halo = 8

    @pl.when(pl.program_id(1) == 0)
    def _():
        for xe in (xq_ref, xk_ref, xv_ref):
            xe[0:halo, :] = jnp.zeros((halo, xe.shape[1]), F32)

    def conv_silu(x_ref, xe, cw_ref):
        xe[halo:halo + tb, :] = x_ref[...].astype(F32)
        first = halo - (GDN_CONV_WIDTH - 1)
        acc = cw_ref[0:1, :] * xe[pl.ds(first, tb), :]
        for j in range(1, GDN_CONV_WIDTH):
            acc = acc + cw_ref[j:j + 1, :] * xe[pl.ds(first + j, tb), :]
        xe[0:halo, :] = xe[tb:tb + halo, :]
        return _silu(acc)

    qa = conv_silu(q_ref, xq_ref, cwq_ref)
    ka = conv_silu(k_ref, xk_ref, cwk_ref)
    va = conv_silu(v_ref, xv_ref, cwv_ref)

    ba = ba_ref[...]
    g_all = -jnp.exp(alog_ref[...]) * _softplus(ba + dt_ref[...])
    beta_all = _sigmoid(ba).reshape(nc, c, LANES)

    ii = lax.broadcasted_iota(jnp.int32, (c, c), 0)
    jj = lax.broadcasted_iota(jnp.int32, (c, c), 1)
    tril = ii >= jj
    strict = ii > jj
    eye = ii == jj
    tril_b = jnp.broadcast_to(tril.astype(BF16)[None], (nc, c, c))
    ones_b = jnp.ones((nc, c, c), BF16)
    gc_all = _bmm_split([tril_b], _split_bf16(g_all.reshape(nc, c, LANES), 3))

    for h in range(heads):
        sl = slice(h * GDN_HEAD_DIM, (h + 1) * GDN_HEAD_DIM)
        qh = qa[:, sl]
        kh = ka[:, sl]
        qh = qh * lax.rsqrt(jnp.sum(qh * qh, axis=-1, keepdims=True) + RMS_EPS) * (GDN_HEAD_DIM ** -0.5)
        kh = kh * lax.rsqrt(jnp.sum(kh * kh, axis=-1, keepdims=True) + RMS_EPS)
        q3 = qh.reshape(nc, c, GDN_HEAD_DIM)
        k3 = kh.reshape(nc, c, GDN_HEAD_DIM)
        v3 = va[:, sl].reshape(nc, c, GDN_HEAD_DIM)
        gcol = gc_all[:, :, heads + h:heads + h + 1]
        bcol = beta_all[:, :, h:h + 1]
        gcol_b = jnp.broadcast_to(gcol, (nc, c, c))
        grow_b = _bmm_split([ones_b], _split_bf16(jnp.where(eye[None], gcol_b, 0.0), 3))
        decay = jnp.where(tril[None], jnp.exp(jnp.where(tril[None], gcol_b - grow_b, 0.0)), 0.0)
        glast = gcol[:, c - 1:c, :]
        eg = jnp.exp(gcol)
        kb = k3 * bcol
        k3b = k3.astype(BF16)
        kk = jnp.einsum('nid,njd->nij', kb.astype(BF16), k3b, preferred_element_type=F32)
        lower = jnp.where(strict[None], kk * decay, 0.0)
        t_inv = _unit_lower_inverse(lower, ii, jj).astype(BF16)
        u = jnp.einsum('nij,njd->nid', t_inv, (v3 * bcol).astype(BF16), preferred_element_type=F32)
        w = jnp.einsum('nij,njd->nid', t_inv, (kb * eg).astype(BF16), preferred_element_type=F32)
        qk = jnp.einsum('nid,njd->nij', q3.astype(BF16), k3b, preferred_element_type=F32) * decay
        u_ref[:, sl] = u.reshape(tb, GDN_HEAD_DIM)
        w_ref[:, sl] = w.reshape(tb, GDN_HEAD_DIM).astype(BF16)
        qd_ref[:, sl] = (q3 * eg).reshape(tb, GDN_HEAD_DIM).astype(BF16)
        kt_ref[:, sl] = (k3 * jnp.exp(glast - gcol)).reshape(tb, GDN_HEAD_DIM).astype(BF16)
        qk_pad = jnp.concatenate([qk, jnp.zeros_like(qk)], axis=-1)
        qk_ref[:, sl] = qk_pad.reshape(tb, GDN_HEAD_DIM).astype(BF16)
        dec_ref[:, sl] = jnp.broadcast_to(jnp.exp(glast), (nc, 8, GDN_HEAD_DIM)).reshape(nc * 8, GDN_HEAD_DIM)


def gdn_prepare(p_gdn, ba, conv_w, a_log, dt_bias, batch, seq, heads):
    gw = heads * GDN_HEAD_DIM
    tb = min(512, seq)
    nc = tb // GDN_CHUNK
    a_row = jnp.zeros((1, LANES), F32).at[0, heads:2 * heads].set(a_log)
    d_row = jnp.zeros((1, LANES), F32).at[0, heads:2 * heads].set(dt_bias)

    def col(j):
        return pl.BlockSpec((None, tb, gw), lambda b, t: (b, t, j))

    def cw(j):
        return pl.BlockSpec((GDN_CONV_WIDTH, gw), lambda b, t: (0, j))

    vec = pl.BlockSpec((1, LANES), lambda b, t: (0, 0))
    tok = pl.BlockSpec((None, tb, gw), lambda b, t: (b, t, 0))
    act = lambda dt: jax.ShapeDtypeStruct((batch, seq, gw), dt)
    kern = functools.partial(_gdn_prep_kernel, heads=heads, tb=tb)
    return pl.pallas_call(
        kern, grid=(batch, seq // tb),
        in_specs=[col(0), col(1), col(2),
                  pl.BlockSpec((None, tb, LANES), lambda b, t: (b, t, 0)),
                  cw(0), cw(1), cw(2), vec, vec],
        out_specs=[tok, tok, tok, tok, tok,
                   pl.BlockSpec((None, nc * 8, gw), lambda b, t: (b, t, 0))],
        out_shape=[act(F32), act(BF16), act(BF16), act(BF16), act(BF16),
                   jax.ShapeDtypeStruct((batch, seq // GDN_CHUNK * 8, gw), F32)],
        scratch_shapes=[pltpu.VMEM((tb + 8, gw), F32)] * 3,
        compiler_params=_params("parallel", "arbitrary"),
    )(p_gdn, p_gdn, p_gdn, ba, conv_w, conv_w, conv_w, a_row, d_row)


def _gdn_scan_kernel(u_ref, w_ref, qd_ref, kt_ref, qk_ref, dec_ref, z_ref, g_ref, o_ref, s_ref,
                     *, batch, heads, tb):
    c = GDN_CHUNK

    @pl.when(pl.program_id(0) == 0)
    def _():
        s_ref[...] = jnp.zeros_like(s_ref)

    def chunk(n, carry):
        r0 = pl.multiple_of(n * c, c)
        rows = pl.ds(r0, c)
        for b in range(batch):
            for h in range(heads):
                sl = slice(h * GDN_HEAD_DIM, (h + 1) * GDN_HEAD_DIM)
                s = s_ref[b * heads + h]
                sb = s.astype(BF16)
                v_new = u_ref[b, rows, sl] - _dot(w_ref[b, rows, sl], sb)
                vb = v_new.astype(BF16)
                qk = qk_ref[b, rows, sl][:, :c]
                o = _dot(qd_ref[b, rows, sl], sb) + _dot(qk, vb)
                ktv = lax.dot_general(kt_ref[b, rows, sl], vb, (((0,), (0,)), ((), ())),
                                      preferred_element_type=F32)
                dec = dec_ref[b, pl.ds(pl.multiple_of(n * 8, 8), 8), sl][0:1, :]
                s_ref[b * heads + h] = s * dec + ktv
                on = o * lax.rsqrt(jnp.mean(o * o, axis=-1, keepdims=True) + RMS_EPS) * g_ref[...]
                z = z_ref[b, rows, sl].astype(F32)
                o_ref[b, rows, sl] = (on * _silu(z)).astype(o_ref.dtype)
        return carry

    lax.fori_loop(0, tb // c, chunk, 0)


def gdn_scan(u, w, qd, kt, qk, dec, p_gdn, norm_g, batch, seq, heads):
    gw = heads * GDN_HEAD_DIM
    tb = min(512, seq)
    nc = tb // GDN_CHUNK
    tok = pl.BlockSpec((batch, tb, gw), lambda t: (0, t, 0))
    kern = functools.partial(_gdn_scan_kernel, batch=batch, heads=heads, tb=tb)
    return pl.pallas_call(
        kern, grid=(seq // tb,),
        in_specs=[tok, tok, tok, tok, tok,
                  pl.BlockSpec((batch, nc * 8, gw), lambda t: (0, t, 0)),
                  pl.BlockSpec((batch, tb, gw), lambda t: (0, t, 3)),
                  pl.BlockSpec((1, GDN_HEAD_DIM), lambda t: (0, 0))],
        out_specs=tok,
        out_shape=jax.ShapeDtypeStruct((batch, seq, gw), BF16),
        scratch_shapes=[pltpu.VMEM((batch * heads, GDN_HEAD_DIM, GDN_HEAD_DIM), F32)],
        compiler_params=_params("arbitrary"),
    )(u, w, qd, kt, qk, dec, p_gdn, norm_g.reshape(1, GDN_HEAD_DIM))


def _conf_kernel(u_ref, w_ref, b_ref, g_ref, beta_ref, o_ref, xe_ref, *, tb, ch, rb):
    halo = 32

    @pl.when(pl.program_id(1) == 0)
    def _():
        xe_ref[0:halo, :] = jnp.zeros((halo, ch), F32)

    u = u_ref[...]
    xe_ref[halo:halo + tb, :] = u[:, :ch].astype(F32) * _sigmoid(u[:, ch:].astype(F32))
    first = halo - (CONF_KERNEL - 1)

    for r in range(tb // rb):
        r0 = r * rb
        acc = jnp.broadcast_to(b_ref[...], (rb, ch))
        for j in range(CONF_KERNEL):
            acc = acc + w_ref[j:j + 1, :] * xe_ref[pl.ds(r0 + first + j, rb), :]
        mu = jnp.mean(acc, axis=-1, keepdims=True)
        xc = acc - mu
        var = jnp.mean(xc * xc, axis=-1, keepdims=True)
        y = xc * lax.rsqrt(var + RMS_EPS) * g_ref[...] + beta_ref[...]
        o_ref[pl.ds(r0, rb), :] = _silu(y).astype(o_ref.dtype)
    xe_ref[0:halo, :] = xe_ref[tb:tb + halo, :]


def conformer_conv(p_conf, dw_w, dw_b, ln_g, ln_b, batch, seq):
    ch = p_conf.shape[-1] // 2
    tb = min(512, seq)
    kern = functools.partial(_conf_kernel, tb=tb, ch=ch, rb=32)
    w_pad = jnp.zeros((32, ch), F32).at[:CONF_KERNEL].set(dw_w)
    vec = pl.BlockSpec((1, ch), lambda b, t: (0, 0))
    return pl.pallas_call(
        kern, grid=(batch, seq // tb),
        in_specs=[pl.BlockSpec((None, tb, 2 * ch), lambda b, t: (b, t, 0)),
                  pl.BlockSpec((32, ch), lambda b, t: (0, 0)), vec, vec, vec],
        out_specs=pl.BlockSpec((None, tb, ch), lambda b, t: (b, t, 0)),
        out_shape=jax.ShapeDtypeStruct((batch, seq, ch), BF16),
        scratch_shapes=[pltpu.VMEM((tb + 32, ch), F32)],
        compiler_params=_params("parallel", "arbitrary"),
    )(p_conf, w_pad, dw_b.reshape(1, ch), ln_g.reshape(1, ch), ln_b.reshape(1, ch))


def _diff_prep_kernel(q_ref, k_ref, ones_ref, gq_ref, gk_ref, qo_ref, ko_ref):
    def norm(x_ref, g_ref, o_ref):
        x = x_ref[...].astype(F32)
        ms = _dot((x * x).astype(BF16), ones_ref[...]) * (1.0 / DIFF_QK_DIM)
        o_ref[...] = (x * lax.rsqrt(ms + RMS_EPS) * g_ref[...]).astype(o_ref.dtype)

    norm(q_ref, gq_ref, qo_ref)
    norm(k_ref, gk_ref, ko_ref)


def diff_prepare(p_diff, q_gain, k_gain, heads):
    n = p_diff.shape[0]
    dw = heads * 2 * DIFF_QK_DIM
    tm = min(512, n)
    grp = jnp.arange(dw) // DIFF_QK_DIM
    ones = (grp[:, None] == grp[None, :]).astype(BF16)
    gq = (jnp.tile(q_gain, 2 * heads) * (DIFF_QK_DIM ** -0.5 * LOG2E)).reshape(1, dw)
    gk = jnp.tile(k_gain, 2 * heads).reshape(1, dw)
    vec = pl.BlockSpec((1, dw), lambda i: (0, 0))
    out = pl.BlockSpec((tm, dw), lambda i: (i, 0))
    return pl.pallas_call(
        _diff_prep_kernel, grid=(n // tm,),
        in_specs=[pl.BlockSpec((tm, dw), lambda i: (i, 0)),
                  pl.BlockSpec((tm, dw), lambda i: (i, 1)),
                  pl.BlockSpec((dw, dw), lambda i: (0, 0)), vec, vec],
        out_specs=[out, out],
        out_shape=[jax.ShapeDtypeStruct((n, dw), BF16)] * 2,
        compiler_params=_params("parallel"),
    )(p_diff, p_diff, ones, gq, gk)


ATTN_HEADS_PER_STEP = 2
ATTN_ROW_BLOCKS = 8


def _diff_attn_kernel(lam_ref, q_ref, k_ref, v_ref, sg_ref, o_ref, loc_ref, m_ref, acc_ref,
                      *, tq, heads, hps, lam_init):
    i = pl.program_id(2)
    d = DIFF_QK_DIM
    dv = DIFF_V_DIM
    lv = lam_ref[...]
    lam = (jnp.exp(jnp.sum(lv[0:1, :] * lv[1:2, :], axis=-1, keepdims=True))
           - jnp.exp(jnp.sum(lv[2:3, :] * lv[3:4, :], axis=-1, keepdims=True)) + lam_init)
    lane = lax.broadcasted_iota(jnp.int32, (tq, LANES), 1)
    rr = lax.broadcasted_iota(jnp.int32, (tq, tq), 0)
    cc = lax.broadcasted_iota(jnp.int32, (tq, tq), 1)
    ones_col = jnp.where(lane == 0, 1.0, 0.0).astype(BF16)
    m_ref[...] = jnp.full(m_ref.shape, -jnp.inf, F32)
    acc_ref[...] = jnp.zeros(acc_ref.shape, F32)

    slopes = []
    q2s = []
    for hh in range(hps):
        sl = slice(hh * LANES, (hh + 1) * LANES)
        h = pl.program_id(1) * hps + hh
        slope = jnp.exp2(jnp.full((1, 1), h + 1, jnp.int32).astype(F32) * (-8.0 / heads)) * LOG2E
        slopes.append(slope)
        q = q_ref[:, sl]
        zero = jnp.zeros_like(q)
        q2s.append(jnp.concatenate([jnp.where(lane < d, q, zero), jnp.where(lane >= d, q, zero)], axis=0))

        @pl.when(i == 0)
        def _(hh=hh, slope=slope):
            local = (cc - rr).astype(F32) * slope
            loc_ref[hh] = jnp.concatenate([local, local], axis=0)

    def step(j, masked):
        start = pl.multiple_of(j * tq, tq)
        rs = 2 * tq // ATTN_ROW_BLOCKS
        for hh in range(hps):
            sl = slice(hh * LANES, (hh + 1) * LANES)
            kj = k_ref[pl.ds(start, tq), sl]
            v2 = jnp.concatenate([v_ref[pl.ds(start, tq), sl], ones_col], axis=1)
            off = slopes[hh] * ((j - i) * tq).astype(F32)
            for r in range(ATTN_ROW_BLOCKS):
                rows = slice(r * rs, (r + 1) * rs)
                s = _dot_nt(q2s[hh][rows], kj) + loc_ref[hh, rows, :]
                if masked:
                    causal = rr >= cc
                    s = jnp.where(jnp.concatenate([causal, causal], axis=0)[rows], s, -jnp.inf)
                m_old = m_ref[hh, rows, :]
                m_new = jnp.maximum(m_old, jnp.max(s, axis=-1, keepdims=True) + off)
                mm = m_new - off
                p = jnp.concatenate([jnp.exp2(s[:, c * LANES:(c + 1) * LANES] - mm)
                                     for c in range(tq // LANES)], axis=1).astype(BF16)
                alpha = jnp.exp2(m_old - m_new)
                acc_old = acc_ref[hh, rows, :]
                acc_ref[hh, rows, :] = (jnp.concatenate([acc_old[:, :dv] * alpha, acc_old[:, dv:] * alpha], axis=1)
                                        + _dot(p, v2))
                m_ref[hh, rows, :] = m_new

    def body(j, carry):
        step(j, False)
        return carry

    lax.fori_loop(0, i, body, 0)
    step(i, True)
    for hh in range(hps):
        acc = acc_ref[hh]
        o = acc[:, :dv] / acc[:, dv:dv + 1]
        out = o[:tq] - lam * o[tq:]
        out = out * lax.rsqrt(jnp.mean(out * out, axis=-1, keepdims=True) + RMS_EPS) * sg_ref[...]
        o_ref[:, hh * LANES:(hh + 1) * LANES] = (out * (1.0 - lam_init)).astype(o_ref.dtype)


def diff_attention(qn, kn, p_diff, lam_vecs, sub_g, batch, seq, heads, lam_init):
    dw = heads * DIFF_V_DIM
    tq = min(512, seq)
    hps = ATTN_HEADS_PER_STEP
    wb = hps * LANES
    lam_pad = jnp.zeros((8, LANES), F32).at[:4, :DIFF_QK_DIM].set(lam_vecs)
    kern = functools.partial(_diff_attn_kernel, tq=tq, heads=heads, hps=hps, lam_init=lam_init)
    return pl.pallas_call(
        kern, grid=(batch, heads // hps, seq // tq),
        in_specs=[pl.BlockSpec((8, LANES), lambda b, h, i: (0, 0)),
                  pl.BlockSpec((None, tq, wb), lambda b, h, i: (b, i, h)),
                  pl.BlockSpec((None, seq, wb), lambda b, h, i: (b, 0, h)),
                  pl.BlockSpec((None, seq, wb), lambda b, h, i: (b, 0, 2 * (heads // hps) + h)),
                  pl.BlockSpec((1, DIFF_V_DIM), lambda b, h, i: (0, 0))],
        out_specs=pl.BlockSpec((None, tq, wb), lambda b, h, i: (b, i, h)),
        out_shape=jax.ShapeDtypeStruct((batch, seq, dw), BF16),
        scratch_shapes=[pltpu.VMEM((hps, 2 * tq, tq), F32), pltpu.VMEM((hps, 2 * tq, LANES), F32),
                        pltpu.VMEM((hps, 2 * tq, 2 * DIFF_V_DIM), F32)],
        compiler_params=_params("parallel", "parallel", "arbitrary"),
    )(lam_pad, qn, kn, p_diff, sub_g.reshape(1, DIFF_V_DIM))


def _merge_kernel(oa_ref, ob_ref, oc_ref, wa_ref, wb_ref, wc_ref, ga_ref, gb_ref, gc_ref, o_ref):
    mixed = _sigmoid(ga_ref[...].astype(F32)) * _dot(oa_ref[...], wa_ref[...])
    mixed = mixed + _sigmoid(gb_ref[...].astype(F32)) * _dot(ob_ref[...], wb_ref[...])
    mixed = mixed + _sigmoid(gc_ref[...].astype(F32)) * _dot(oc_ref[...], wc_ref[...])
    o_ref[...] = mixed.astype(o_ref.dtype)


def merge_branches(oa, ob, oc, wa, wb, wc, gates, d):
    n = oa.shape[0]
    tm = min(512, n)
    tn = min(512, d)
    nj = d // tn

    def lhs(x):
        return pl.BlockSpec((tm, x.shape[1]), lambda i, j: (i, 0))

    def rhs(w):
        return pl.BlockSpec((w.shape[0], tn), lambda i, j: (0, j))

    def gate(k):
        return pl.BlockSpec((tm, tn), lambda i, j: (i, k * nj + j))

    return pl.pallas_call(
        _merge_kernel, grid=(n // tm, nj),
        in_specs=[lhs(oa), lhs(ob), lhs(oc), rhs(wa), rhs(wb), rhs(wc), gate(0), gate(1), gate(2)],
        out_specs=pl.BlockSpec((tm, tn), lambda i, j: (i, j)),
        out_shape=jax.ShapeDtypeStruct((n, d), BF16),
        compiler_params=_params("parallel", "parallel"),
    )(oa, ob, oc, wa, wb, wc, gates, gates, gates)


def _proj_resid_kernel(a_ref, w_ref, x_ref, gate_ref, o_ref):
    o_ref[...] = x_ref[...] + gate_ref[...] * _dot(a_ref[...], w_ref[...])


def project_residual(a, w, x, mod4, gate_chunk, seq):
    n, k = a.shape
    d = w.shape[1]
    tm = min(1024, seq)
    tn = min(512, d)
    tpb = seq // tm
    return pl.pallas_call(
        _proj_resid_kernel, grid=(n // tm, d // tn),
        in_specs=[pl.BlockSpec((tm, k), lambda i, j: (i, 0)),
                  pl.BlockSpec((k, tn), lambda i, j: (0, j)),
                  pl.BlockSpec((tm, tn), lambda i, j: (i, j)),
                  pl.BlockSpec((None, None, 1, tn), lambda i, j: (i // tpb, gate_chunk, 0, j))],
        out_specs=pl.BlockSpec((tm, tn), lambda i, j: (i, j)),
        out_shape=jax.ShapeDtypeStruct((n, d), F32),
        compiler_params=_params("parallel", "parallel"),
    )(a, w, x, mod4)


def _route_rows(sig, sel):
    per_group = N_EXPERTS // N_GROUPS
    neg = jnp.full_like(sel[0], -jnp.inf)
    group_scores = []
    for g in range(N_GROUPS):
        r = sel[g * per_group:(g + 1) * per_group]
        best = None
        for a in range(per_group):
            for b in range(a + 1, per_group):
                pair = r[a] + r[b]
                best = pair if best is None else jnp.maximum(best, pair)
        group_scores.append(best)
    best_g = jnp.zeros_like(sel[0], dtype=jnp.int32)
    best_s = group_scores[0]
    for g in range(1, N_GROUPS):
        upd = group_scores[g] > best_s
        best_g = jnp.where(upd, g, best_g)
        best_s = jnp.where(upd, group_scores[g], best_s)
    masked = [jnp.where(best_g == (e // per_group), sel[e], neg) for e in range(N_EXPERTS)]

    def argmax_rows(rows):
        idx = jnp.zeros_like(best_g)
        val = rows[0]
        for e in range(1, N_EXPERTS):
            upd = rows[e] > val
            idx = jnp.where(upd, e, idx)
            val = jnp.where(upd, rows[e], val)
        return idx

    idx1 = argmax_rows(masked)
    idx2 = argmax_rows([jnp.where(idx1 == e, neg, masked[e]) for e in range(N_EXPERTS)])
    zero = jnp.zeros_like(sel[0])
    w1 = zero
    w2 = zero
    for e in range(N_EXPERTS):
        w1 = w1 + jnp.where(idx1 == e, sig[e], zero)
        w2 = w2 + jnp.where(idx2 == e, sig[e], zero)
    inv = 1.0 / (w1 + w2)
    return idx1, idx2, w1 * inv, w2 * inv


def _norm_route_kernel(x_ref, g_ref, sc_ref, sh_ref, rw_ref, rb_ref, tri_ref,
                       h_ref, ids_ref, wts_ref, cum_ref, carry_ref):
    @pl.when(pl.program_id(0) == 0)
    def _():
        carry_ref[...] = jnp.zeros_like(carry_ref)

    h = _norm_mod(x_ref[...], g_ref[...], sc_ref[...], sh_ref[...])
    h_ref[...] = h.astype(h_ref.dtype)
    logits = lax.dot_general(rw_ref[...], h, (((1,), (1,)), ((), ())), precision=HIGHEST,
                             preferred_element_type=F32)
    sig_all = _sigmoid(logits)
    sel_all = sig_all + rb_ref[...]
    sig = [sig_all[e:e + 1, :] for e in range(N_EXPERTS)]
    sel = [sel_all[e:e + 1, :] for e in range(N_EXPERTS)]
    idx1, idx2, w1, w2 = _route_rows(sig, sel)
    onehot = jnp.concatenate(
        [jnp.where((idx1 == e) | (idx2 == e), 1.0, 0.0) for e in range(N_EXPERTS)], axis=0)
    cum = _dot(onehot.astype(BF16), tri_ref[...]) + carry_ref[...]
    tm = cum.shape[1]
    carry_ref[...] = cum[:, tm - 1:tm]
    cum_ref[...] = cum
    zero = jnp.zeros_like(w1)
    rank1 = zero
    rank2 = zero
    for e in range(N_EXPERTS):
        rank1 = rank1 + jnp.where(idx1 == e, cum[e:e + 1, :], zero)
        rank2 = rank2 + jnp.where(idx2 == e, cum[e:e + 1, :], zero)
    izero = jnp.zeros_like(idx1)
    ids_ref[...] = jnp.concatenate([idx1, idx2, (rank1 - 1.0).astype(jnp.int32),
                                    (rank2 - 1.0).astype(jnp.int32)] + [izero] * 4, axis=0)
    wts_ref[...] = jnp.concatenate([w1, w2] + [zero] * 6, axis=0)


def norm_route(x, g, mod4, scale_chunk, shift_chunk, router_w, router_bias, seq):
    n, d = x.shape
    tm = min(256, seq)
    tpb = seq // tm
    row = pl.BlockSpec((tm, d), lambda i: (i, 0))
    vec = pl.BlockSpec((1, d), lambda i: (0, 0))
    tri = (jnp.arange(tm)[:, None] <= jnp.arange(tm)[None, :]).astype(BF16)
    info = pl.BlockSpec((8, tm), lambda i: (0, i))
    return pl.pallas_call(
        _norm_route_kernel, grid=(n // tm,),
        in_specs=[row, vec, _mod_spec(d, tpb, scale_chunk), _mod_spec(d, tpb, shift_chunk),
                  pl.BlockSpec((N_EXPERTS, d), lambda i: (0, 0)),
                  pl.BlockSpec((N_EXPERTS, 1), lambda i: (0, 0)),
                  pl.BlockSpec((tm, tm), lambda i: (0, 0))],
        out_specs=[row, info, info, pl.BlockSpec((N_EXPERTS, tm), lambda i: (0, i))],
        out_shape=[jax.ShapeDtypeStruct((n, d), BF16), jax.ShapeDtypeStruct((8, n), jnp.int32),
                   jax.ShapeDtypeStruct((8, n), F32), jax.ShapeDtypeStruct((N_EXPERTS, n), F32)],
        scratch_shapes=[pltpu.VMEM((N_EXPERTS, 1), F32)],
        compiler_params=_params("arbitrary"),
    )(x, g.reshape(1, d), mod4, mod4, router_w.T, router_bias.reshape(N_EXPERTS, 1), tri)


MOE_ROW_TILE = 256
MOE_FFN_TILE = 512
MOE_TOK_CHUNK = 512


def _count_le(sorted_vals, x):
    return jnp.sum(sorted_vals[None, :] <= x[:, None], axis=1).astype(jnp.int32)


def moe_plan(ids, cum, n):
    tr, tf, tc = MOE_ROW_TILE, MOE_FFN_TILE, MOE_TOK_CHUNK
    tc = min(tc, n)
    e1, e2, r1, r2 = ids[0], ids[1], ids[2], ids[3]
    cnt = cum[:, n - 1].astype(jnp.int32)
    n_rt = 2 * n // tr + N_EXPERTS * (tf // tr)
    n_ft = n_rt * tr // tf
    rt_per_e = ((cnt + tf - 1) // tf) * (tf // tr)
    rt_end = jnp.cumsum(rt_per_e)
    rt_start = rt_end - rt_per_e
    used_rt = rt_end[-1]
    row_start = rt_start * tr
    d1 = row_start[e1] + r1
    d2 = row_start[e2] + r2
    rt = jnp.arange(n_rt, dtype=jnp.int32)
    rt_exp = jnp.clip(_count_le(rt_end, rt), 0, N_EXPERTS - 1)
    ft = jnp.arange(n_ft, dtype=jnp.int32)
    ft_exp = rt_exp[ft * (tf // tr)]
    used_ft = used_rt // (tf // tr)
    chunk_cum = cum[:, tc - 1::tc].astype(jnp.int32)
    n_chunks = n // tc
    a = (rt - rt_start[rt_exp]) * tr
    b = jnp.minimum(a + tr, cnt[rt_exp])
    cc = chunk_cum[rt_exp]
    lo_c = jnp.sum(cc <= a[:, None], axis=1)
    hi_c = jnp.sum(cc < b[:, None], axis=1)
    lo_c = jnp.clip(lo_c, 0, n_chunks - 1)
    hi_c = jnp.clip(jnp.maximum(hi_c, lo_c), 0, n_chunks - 1)
    n_c = jnp.where(rt < used_rt, hi_c - lo_c + 1, 1)
    item_end = jnp.cumsum(n_c)
    item_start = item_end - n_c
    total = item_end[-1]
    n_items = n_rt + N_EXPERTS * n_chunks
    w = jnp.arange(n_items, dtype=jnp.int32)
    valid = w < total
    wl = jnp.minimum(w, total - 1)
    wt = jnp.clip(_count_le(item_end, wl), 0, n_rt - 1)
    wc = (lo_c[wt] + (wl - item_start[wt])).astype(jnp.int32)
    g_first = (valid & (w == item_start[wt])).astype(jnp.int32)
    g_last = (valid & (w == item_end[wt] - 1)).astype(jnp.int32)
    work = valid & (wt < used_rt)
    gather_items = (wt, wc, g_first, g_last, work.astype(jnp.int32))
    key = jnp.where(work, wc * n_rt + wt, jnp.iinfo(jnp.int32).max)
    before = (key[None, :] < key[:, None]) | ((key[None, :] == key[:, None]) & (w[None, :] < w[:, None]))
    pos = jnp.sum(before, axis=1).astype(jnp.int32)
    place = pos[:, None] == w[None, :]
    st = jnp.sum(jnp.where(place, wt[:, None], 0), axis=0).astype(jnp.int32)
    sc = jnp.sum(jnp.where(place, wc[:, None], 0), axis=0).astype(jnp.int32)
    n_work = jnp.sum(work).astype(jnp.int32)
    sv = w < n_work
    last_valid = jnp.maximum(n_work - 1, 0)
    st = jnp.where(sv, st, st[last_valid])
    sc = jnp.where(sv, sc, sc[last_valid])
    prev_c = jnp.concatenate([jnp.full((1,), -1, jnp.int32), sc[:-1]])
    next_c = jnp.concatenate([sc[1:], jnp.full((1,), -1, jnp.int32)])
    next_v = jnp.concatenate([sv[1:], jnp.zeros((1,), bool)])
    s_first = (sv & (sc != prev_c)).astype(jnp.int32)
    s_last = (sv & ((sc != next_c) | ~next_v)).astype(jnp.int32)
    scatter_items = (st, sc, s_first, s_last, sv.astype(jnp.int32))
    return dict(d1=d1, d2=d2, n_rt=n_rt, n_ft=n_ft, ft_exp=ft_exp, used_ft=used_ft.astype(jnp.int32),
                gather_items=gather_items, scatter_items=scatter_items, n_items=n_items, tc=tc)


def _moe_gather_kernel(tile_ref, chunk_ref, first_ref, last_ref, valid_ref,
                       h_ref, dest_ref, wts_ref, xs_ref, ws_ref, acc_ref, wacc_ref):
    w = pl.program_id(0)
    tr = xs_ref.shape[0]

    @pl.when(first_ref[w] == 1)
    def _():
        acc_ref[...] = jnp.zeros_like(acc_ref)
        wacc_ref[...] = jnp.zeros_like(wacc_ref)

    @pl.when(valid_ref[w] == 1)
    def _():
        rows = tile_ref[w] * tr + lax.broadcasted_iota(jnp.int32, (tr, 1), 0)
        hit1 = dest_ref[0:1, :] == rows
        hit2 = dest_ref[1:2, :] == rows
        onehot = jnp.where(hit1 | hit2, 1.0, 0.0).astype(BF16)
        acc_ref[...] += _dot(onehot, h_ref[...])
        pw = jnp.where(hit1, wts_ref[0:1, :], 0.0) + jnp.where(hit2, wts_ref[1:2, :], 0.0)
        wacc_ref[...] += jnp.sum(pw, axis=1, keepdims=True)

    @pl.when(last_ref[w] == 1)
    def _():
        xs_ref[...] = acc_ref[...].astype(xs_ref.dtype)
        ws_ref[...] = wacc_ref[...]


def moe_gather(h2, dest, wts, plan):
    n, d = h2.shape
    tr, tc = MOE_ROW_TILE, plan['tc']
    n_rows = plan['n_rt'] * tr
    grid_spec = pltpu.PrefetchScalarGridSpec(
        num_scalar_prefetch=5, grid=(plan['n_items'],),
        in_specs=[pl.BlockSpec((tc, d), lambda w, t, c, f, l, v: (c[w], 0)),
                  pl.BlockSpec((2, tc), lambda w, t, c, f, l, v: (0, c[w])),
                  pl.BlockSpec((8, tc), lambda w, t, c, f, l, v: (0, c[w]))],
        out_specs=[pl.BlockSpec((tr, d), lambda w, t, c, f, l, v: (t[w], 0)),
                   pl.BlockSpec((tr, 1), lambda w, t, c, f, l, v: (t[w], 0))],
        scratch_shapes=[pltpu.VMEM((tr, d), F32), pltpu.VMEM((tr, 1), F32)])
    return pl.pallas_call(
        _moe_gather_kernel, grid_spec=grid_spec,
        out_shape=[jax.ShapeDtypeStruct((n_rows, d), BF16), jax.ShapeDtypeStruct((n_rows, 1), F32)],
        compiler_params=_params("arbitrary"),
    )(*plan['gather_items'], h2, dest, wts)


def _moe_ffn_kernel(exp_ref, used_ref, x_ref, wg_ref, wu_ref, wd_ref, ws_ref, o_ref, acc_ref, *, n_f):
    i = pl.program_id(0)
    f = pl.program_id(1)
    used = i < used_ref[0]

    @pl.when(f == 0)
    def _():
        acc_ref[...] = jnp.zeros_like(acc_ref)

    @pl.when(used)
    def _():
        x = x_ref[...]
        hid = _silu(_dot(x, wg_ref[...])) * _dot(x, wu_ref[...])
        acc_ref[...] += _dot((hid * ws_ref[...]).astype(BF16), wd_ref[...])

    @pl.when(f == n_f - 1)
    def _():
        o_ref[...] = acc_ref[...].astype(o_ref.dtype)


def moe_ffn(xs, ws, wg, wu, wd, plan):
    n_rows, d = xs.shape
    ff = wg.shape[-1]
    tm = MOE_FFN_TILE
    tf = min(512, ff)
    n_f = ff // tf
    kern = functools.partial(_moe_ffn_kernel, n_f=n_f)

    def row(i, f, e, u):
        return (jnp.minimum(i, u[0] - 1), 0)

    grid_spec = pltpu.PrefetchScalarGridSpec(
        num_scalar_prefetch=2, grid=(plan['n_ft'], n_f),
        in_specs=[pl.BlockSpec((tm, d), row),
                  pl.BlockSpec((None, d, tf), lambda i, f, e, u: (e[i], 0, f)),
                  pl.BlockSpec((None, d, tf), lambda i, f, e, u: (e[i], 0, f)),
                  pl.BlockSpec((None, tf, d), lambda i, f, e, u: (e[i], f, 0)),
                  pl.BlockSpec((tm, 1), row)],
        out_specs=pl.BlockSpec((tm, d), lambda i, f, e, u: (i, 0)),
        scratch_shapes=[pltpu.VMEM((tm, d), F32)])
    return pl.pallas_call(
        kern, grid_spec=grid_spec,
        out_shape=jax.ShapeDtypeStruct((n_rows, d), BF16),
        compiler_params=_params("arbitrary", "arbitrary"),
    )(plan['ft_exp'], plan['used_ft'].reshape(1), xs, wg, wu, wd, ws)


def _moe_scatter_kernel(tile_ref, chunk_ref, first_ref, last_ref, valid_ref,
                        ys_ref, dest_ref, y_ref, acc_ref):
    w = pl.program_id(0)
    tr = ys_ref.shape[0]

    @pl.when(first_ref[w] == 1)
    def _():
        acc_ref[...] = jnp.zeros_like(acc_ref)

    @pl.when(valid_ref[w] == 1)
    def _():
        rows = tile_ref[w] * tr + lax.broadcasted_iota(jnp.int32, (1, tr), 1)
        hit = (dest_ref[:, 0:1] == rows) | (dest_ref[:, 1:2] == rows)
        acc_ref[...] += _dot(jnp.where(hit, 1.0, 0.0).astype(BF16), ys_ref[...])

    @pl.when(last_ref[w] == 1)
    def _():
        y_ref[...] = acc_ref[...].astype(y_ref.dtype)


def moe_scatter(ys, dest_t, plan, n):
    d = ys.shape[1]
    tr, tc = MOE_ROW_TILE, plan['tc']
    grid_spec = pltpu.PrefetchScalarGridSpec(
        num_scalar_prefetch=5, grid=(plan['n_items'],),
        in_specs=[pl.BlockSpec((tr, d), lambda w, t, c, f, l, v: (t[w], 0)),
                  pl.BlockSpec((tc, 2), lambda w, t, c, f, l, v: (c[w], 0))],
        out_specs=pl.BlockSpec((tc, d), lambda w, t, c, f, l, v: (c[w], 0)),
        scratch_shapes=[pltpu.VMEM((tc, d), F32)])
    return pl.pallas_call(
        _moe_scatter_kernel, grid_spec=grid_spec,
        out_shape=jax.ShapeDtypeStruct((n, d), BF16),
        compiler_params=_params("arbitrary"),
    )(*plan['scatter_items'], ys, dest_t)


def moe_sparse(h2, ids, wts, cum, wg, wu, wd):
    n = h2.shape[0]
    plan = moe_plan(ids, cum, n)
    dest = jnp.stack([plan['d1'], plan['d2']])
    xs, ws = moe_gather(h2, dest, wts, plan)
    ys = moe_ffn(xs, ws, wg, wu, wd, plan)
    return moe_scatter(ys, dest.T, plan, n)


def kernel(x, c, ada_w, ada_b, norm1_g, w_in, gdn_conv_w, gdn_a_log, gdn_dt_bias, gdn_norm_g, gdn_w_out, conf_dw_w, conf_dw_b, conf_ln_g, conf_ln_b, conf_w_out, diff_q_norm_g, diff_k_norm_g, diff_lambda_q1, diff_lambda_k1, diff_lambda_q2, diff_lambda_k2, diff_sub_g, diff_w_out, w_o, norm2_g, router_w, router_bias, exp_w_gate, exp_w_up, exp_w_down):
    batch, seq, d = x.shape
    n = batch * seq
    depth = ada_w.shape[0]
    heads = d // D_PER_HEAD
    gw = heads * GDN_HEAD_DIM
    dw = heads * DIFF_V_DIM
    conf_ch = conf_dw_w.shape[-1]
    o_ba = 4 * gw
    o_conf = o_ba + 2 * heads
    o_diff = o_conf + 2 * conf_ch
    o_gate = o_diff + 3 * dw

    mod = ada_modulation(c, ada_w, ada_b)
    xf = x.reshape(n, d)
    y_prev = None
    mod4_prev = None
    for l in range(depth):
        mod4 = mod[l].reshape(batch, ADA_CHUNKS, 1, d)
        if y_prev is None:
            h = norm_modulate(xf, norm1_g[l], mod4, 1, 0, seq)
        else:
            xf, h = norm_modulate(xf, norm1_g[l], mod4, 1, 0, seq,
                                  y=y_prev, gate_mod4=mod4_prev, gate_chunk=5)
        wl = w_in[l]
        w_ba = jnp.zeros((d, LANES), BF16).at[:, :2 * heads].set(wl[:, o_ba:o_conf].astype(BF16))
        p_gdn = matmul(h, wl[:, :o_ba].astype(BF16), BF16).reshape(batch, seq, 4 * gw)
        ba = matmul(h, w_ba, F32).reshape(batch, seq, LANES)
        p_conf = matmul(h, wl[:, o_conf:o_diff].astype(BF16), BF16).reshape(batch, seq, 2 * conf_ch)
        p_diff = matmul(h, wl[:, o_diff:o_gate].astype(BF16), BF16)
        gates = matmul(h, wl[:, o_gate:].astype(BF16), BF16)

        u, w, qd, kt, qk, dec = gdn_prepare(p_gdn, ba, gdn_conv_w[l], gdn_a_log[l], gdn_dt_bias[l],
                                            batch, seq, heads)
        o_a = gdn_scan(u, w, qd, kt, qk, dec, p_gdn, gdn_norm_g[l], batch, seq, heads)
        o_b = conformer_conv(p_conf, conf_dw_w[l], conf_dw_b[l], conf_ln_g[l], conf_ln_b[l], batch, seq)
        qn, kn = diff_prepare(p_diff, diff_q_norm_g[l], diff_k_norm_g[l], heads)
        lam_init = 0.8 - 0.6 * math.exp(-0.3 * l)
        lam_vecs = jnp.stack([diff_lambda_q1[l], diff_lambda_k1[l], diff_lambda_q2[l], diff_lambda_k2[l]])
        o_c = diff_attention(qn.reshape(batch, seq, dw), kn.reshape(batch, seq, dw),
                             p_diff.reshape(batch, seq, 3 * dw), lam_vecs, diff_sub_g[l],
                             batch, seq, heads, lam_init)
        mixed = merge_branches(o_a.reshape(n, gw), o_b.reshape(n, conf_ch), o_c.reshape(n, dw),
                               gdn_w_out[l].astype(BF16), conf_w_out[l].astype(BF16),
                               diff_w_out[l].astype(BF16), gates, d)
        xf = project_residual(mixed, w_o[l].astype(BF16), xf, mod4, 2, seq)

        h2, ids, wts, cum = norm_route(xf, norm2_g[l], mod4, 4, 3, router_w, router_bias, seq)
        y_prev = moe_sparse(h2, ids, wts, cum, exp_w_gate[l].astype(BF16), exp_w_up[l].astype(BF16),
                            exp_w_down[l].astype(BF16))
        mod4_prev = mod4
    xf = gated_residual(xf, y_prev, mod4_prev, 5, seq)
    return xf.reshape(batch, seq, d)
```

```python
import functools
import math

import jax
import jax.numpy as jnp
from jax import lax
from jax.experimental import pallas as pl
from jax.experimental.pallas import tpu as pltpu

F32 = jnp.float32
BF16 = jnp.bfloat16
HIGHEST = lax.Precision.HIGHEST
LOG2E = math.log2(math.e)

RMS_EPS = 1e-6
LANES = 128
GDN_HEAD_DIM = 128
GDN_CONV_WIDTH = 4
GDN_CHUNK = 64
CONF_KERNEL = 31
DIFF_QK_DIM = 64
DIFF_V_DIM = 128
N_EXPERTS = 16
N_GROUPS = 4
ADA_CHUNKS = 6
D_PER_HEAD = 512
VMEM_LIMIT_BYTES = 56 * 1024 * 1024


def _params(*semantics):
    return pltpu.CompilerParams(dimension_semantics=semantics,
                                vmem_limit_bytes=VMEM_LIMIT_BYTES)


def _sigmoid(x):
    return 1.0 / (1.0 + jnp.exp(-x))


def _silu(x):
    return x * _sigmoid(x)


def _softplus(x):
    return jnp.maximum(x, 0.0) + jnp.log(1.0 + jnp.exp(-jnp.abs(x)))


def _dot(a, b):
    return jnp.dot(a, b, preferred_element_type=F32)


def _dot_nt(a, b):
    return lax.dot_general(a, b, (((1,), (1,)), ((), ())), preferred_element_type=F32)


def _ada_kernel(c_ref, w_ref, b_ref, o_ref):
    cond = _silu(c_ref[...]).astype(BF16)
    o_ref[...] = _dot(cond, w_ref[...].astype(BF16)) + b_ref[...]


def ada_modulation(c, ada_w, ada_b):
    n_layers, d, d6 = ada_w.shape
    b = c.shape[0]
    rows = 8
    c_pad = jnp.zeros((rows, d), F32).at[:b].set(c)
    tn = min(512, d6)
    out = pl.pallas_call(
        _ada_kernel,
        grid=(n_layers, d6 // tn),
        in_specs=[pl.BlockSpec((rows, d), lambda l, j: (0, 0)),
                  pl.BlockSpec((None, d, tn), lambda l, j: (l, 0, j)),
                  pl.BlockSpec((None, 1, tn), lambda l, j: (l, 0, j))],
        out_specs=pl.BlockSpec((None, rows, tn), lambda l, j: (l, 0, j)),
        out_shape=jax.ShapeDtypeStruct((n_layers, rows, d6), F32),
        compiler_params=_params("parallel", "parallel"),
    )(c_pad, ada_w, ada_b.reshape(n_layers, 1, d6))
    return out[:, :b]


def _norm_mod(x, g, scale, shift):
    ms = jnp.mean(x * x, axis=-1, keepdims=True)
    y = x * lax.rsqrt(ms + RMS_EPS) * g
    return y * (1.0 + scale) + shift


def _normmod_kernel(x_ref, g_ref, sc_ref, sh_ref, h_ref):
    h_ref[...] = _norm_mod(x_ref[...], g_ref[...], sc_ref[...], sh_ref[...]).astype(h_ref.dtype)


def _resid_normmod_kernel(x_ref, y_ref, gate_ref, g_ref, sc_ref, sh_ref, xo_ref, h_ref):
    x = x_ref[...] + gate_ref[...] * y_ref[...].astype(F32)
    xo_ref[...] = x
    h_ref[...] = _norm_mod(x, g_ref[...], sc_ref[...], sh_ref[...]).astype(h_ref.dtype)


def _resid_kernel(x_ref, y_ref, gate_ref, xo_ref):
    xo_ref[...] = x_ref[...] + gate_ref[...] * y_ref[...].astype(F32)


def _mod_spec(d, tiles_per_batch, chunk):
    return pl.BlockSpec((None, None, 1, d), lambda i: (i // tiles_per_batch, chunk, 0, 0))


def norm_modulate(x, g, mod4, scale_chunk, shift_chunk, seq, y=None, gate_mod4=None, gate_chunk=None):
    n, d = x.shape
    tm = min(256, seq)
    tpb = seq // tm
    row = pl.BlockSpec((tm, d), lambda i: (i, 0))
    vec = pl.BlockSpec((1, d), lambda i: (0, 0))
    h_shape = jax.ShapeDtypeStruct((n, d), BF16)
    if y is None:
        return pl.pallas_call(
            _normmod_kernel, grid=(n // tm,),
            in_specs=[row, vec, _mod_spec(d, tpb, scale_chunk), _mod_spec(d, tpb, shift_chunk)],
            out_specs=row, out_shape=h_shape, compiler_params=_params("parallel"),
        )(x, g.reshape(1, d), mod4, mod4)
    return pl.pallas_call(
        _resid_normmod_kernel, grid=(n // tm,),
        in_specs=[row, row, _mod_spec(d, tpb, gate_chunk), vec,
                  _mod_spec(d, tpb, scale_chunk), _mod_spec(d, tpb, shift_chunk)],
        out_specs=[row, row],
        out_shape=[jax.ShapeDtypeStruct((n, d), F32), h_shape],
        compiler_params=_params("parallel"),
    )(x, y, gate_mod4, g.reshape(1, d), mod4, mod4)


def gated_residual(x, y, mod4, gate_chunk, seq):
    n, d = x.shape
    tm = min(256, seq)
    tpb = seq // tm
    row = pl.BlockSpec((tm, d), lambda i: (i, 0))
    return pl.pallas_call(
        _resid_kernel, grid=(n // tm,),
        in_specs=[row, row, _mod_spec(d, tpb, gate_chunk)],
        out_specs=row, out_shape=jax.ShapeDtypeStruct((n, d), F32),
        compiler_params=_params("parallel"),
    )(x, y, mod4)


def _mm_kernel(a_ref, b_ref, o_ref):
    o_ref[...] = _dot(a_ref[...], b_ref[...]).astype(o_ref.dtype)


def matmul(a, b, out_dtype, tm=1024, tn=512):
    m, k = a.shape
    n = b.shape[1]
    tm = min(tm, m)
    tn = min(tn, n)
    while n % tn:
        tn -= LANES
    return pl.pallas_call(
        _mm_kernel, grid=(m // tm, n // tn),
        in_specs=[pl.BlockSpec((tm, k), lambda i, j: (i, 0)),
                  pl.BlockSpec((k, tn), lambda i, j: (0, j))],
        out_specs=pl.BlockSpec((tm, tn), lambda i, j: (i, j)),
        out_shape=jax.ShapeDtypeStruct((m, n), out_dtype),
        compiler_params=_params("parallel", "parallel"),
    )(a, b)


def _unit_lower_inverse(lower, ii, jj):
    t = jnp.broadcast_to((ii == jj).astype(F32)[None], lower.shape)
    s = 1
    while s < GDN_CHUNK:
        sh = s.bit_length() - 1
        sel = ((ii >> (sh + 1)) == (jj >> (sh + 1))) & ((ii >> sh) != (jj >> sh)) & (ii > jj)
        lo = jnp.where(sel[None], lower, 0.0)
        t_parts = _split_bf16(t, 2)
        tl = _bmm_split(t_parts, _split_bf16(lo, 2))
        t = t - _bmm_split(_split_bf16(tl, 2), t_parts)
        s *= 2
    return t


def _split_bf16(x, n):
    parts = []
    for _ in range(n - 1):
        p = x.astype(BF16)
        parts.append(p)
        x = x - p.astype(F32)
    parts.append(x.astype(BF16))
    return parts


def _bmm_split(a_parts, b_parts):
    out = None
    for ia, a in enumerate(a_parts):
        for ib, b in enumerate(b_parts):
            if ia + ib >= max(len(a_parts), len(b_parts)):
                continue
            term = jnp.einsum('nij,njk->nik', a, b, preferred_element_type=F32)
            out = term if out is None else out + term
    return out


def _gdn_prep_kernel(q_ref, k_ref, v_ref, ba_ref, cwq_ref, cwk_ref, cwv_ref, alog_ref, dt_ref,
                     u_ref, w_ref, qd_ref, kt_ref, qk_ref, dec_ref,
                     xq_ref, xk_ref, xv_ref, *, heads, tb):
    c = GDN_CHUNK
    nc = tb // c
    halo = 8

    @pl.when(pl.program_id(1) == 0)
    def _():
        for xe in (xq_ref, xk_ref, xv_ref):
            xe[0:halo, :] = jnp.zeros((halo, xe.shape[1]), F32)

    def conv_silu(x_ref, xe, cw_ref):
        xe[halo:halo + tb, :] = x_ref[...].astype(F32)
        first = halo - (GDN_CONV_WIDTH - 1)
        acc = cw_ref[0:1, :] * xe[pl.ds(first, tb), :]
        for j in range(1, GDN_CONV_WIDTH):
            acc = acc + cw_ref[j:j + 1, :] * xe[pl.ds(first + j, tb), :]
        xe[0:halo, :] = xe[tb:tb + halo, :]
        return _silu(acc)

    qa = conv_silu(q_ref, xq_ref, cwq_ref)
    ka = conv_silu(k_ref, xk_ref, cwk_ref)
    va = conv_silu(v_ref, xv_ref, cwv_ref)

    ba = ba_ref[...]
    g_all = -jnp.exp(alog_ref[...]) * _softplus(ba + dt_ref[...])
    beta_all = _sigmoid(ba).reshape(nc, c, LANES)

    ii = lax.broadcasted_iota(jnp.int32, (c, c), 0)
    jj = lax.broadcasted_iota(jnp.int32, (c, c), 1)
    tril = ii >= jj
    strict = ii > jj
    eye = ii == jj
    tril_b = jnp.broadcast_to(tril.astype(BF16)[None], (nc, c, c))
    ones_b = jnp.ones((nc, c, c), BF16)
    gc_all = _bmm_split([tril_b], _split_bf16(g_all.reshape(nc, c, LANES), 3))

    for h in range(heads):
        sl = slice(h * GDN_HEAD_DIM, (h + 1) * GDN_HEAD_DIM)
        qh = qa[:, sl]
        kh = ka[:, sl]
        qh = qh * lax.rsqrt(jnp.sum(qh * qh, axis=-1, keepdims=True) + RMS_EPS) * (GDN_HEAD_DIM ** -0.5)
        kh = kh * lax.rsqrt(jnp.sum(kh * kh, axis=-1, keepdims=True) + RMS_EPS)
        q3 = qh.reshape(nc, c, GDN_HEAD_DIM)
        k3 = kh.reshape(nc, c, GDN_HEAD_DIM)
        v3 = va[:, sl].reshape(nc, c, GDN_HEAD_DIM)
        gcol = gc_all[:, :, heads + h:heads + h + 1]
        bcol = beta_all[:, :, h:h + 1]
        gcol_b = jnp.broadcast_to(gcol, (nc, c, c))
        grow_b = _bmm_split([ones_b], _split_bf16(jnp.where(eye[None], gcol_b, 0.0), 3))
        decay = jnp.where(tril[None], jnp.exp(jnp.where(tril[None], gcol_b - grow_b, 0.0)), 0.0)
        glast = gcol[:, c - 1:c, :]
        eg = jnp.exp(gcol)
        kb = k3 * bcol
        k3b = k3.astype(BF16)
        kk = jnp.einsum('nid,njd->nij', kb.astype(BF16), k3b, preferred_element_type=F32)
        lower = jnp.where(strict[None], kk * decay, 0.0)
        t_inv = _unit_lower_inverse(lower, ii, jj).astype(BF16)
        u = jnp.einsum('nij,njd->nid', t_inv, (v3 * bcol).astype(BF16), preferred_element_type=F32)
        w = jnp.einsum('nij,njd->nid', t_inv, (kb * eg).astype(BF16), preferred_element_type=F32)
        qk = jnp.einsum('nid,njd->nij', q3.astype(BF16), k3b, preferred_element_type=F32) * decay
        u_ref[:, sl] = u.reshape(tb, GDN_HEAD_DIM)
        w_ref[:, sl] = w.reshape(tb, GDN_HEAD_DIM).astype(BF16)
        qd_ref[:, sl] = (q3 * eg).reshape(tb, GDN_HEAD_DIM).astype(BF16)
        kt_ref[:, sl] = (k3 * jnp.exp(glast - gcol)).reshape(tb, GDN_HEAD_DIM).astype(BF16)
        qk_pad = jnp.concatenate([qk, jnp.zeros_like(qk)], axis=-1)
        qk_ref[:, sl] = qk_pad.reshape(tb, GDN_HEAD_DIM).astype(BF16)
        dec_ref[:, sl] = jnp.broadcast_to(jnp.exp(glast), (nc, 8, GDN_HEAD_DIM)).reshape(nc * 8, GDN_HEAD_DIM)


def gdn_prepare(p_gdn, ba, conv_w, a_log, dt_bias, batch, seq, heads):
    gw = heads * GDN_HEAD_DIM
    tb = min(512, seq)
    nc = tb // GDN_CHUNK
    a_row = jnp.zeros((1, LANES), F32).at[0, heads:2 * heads].set(a_log)
    d_row = jnp.zeros((1, LANES), F32).at[0, heads:2 * heads].set(dt_bias)

    def col(j):
        return pl.BlockSpec((None, tb, gw), lambda b, t: (b, t, j))

    def cw(j):
        return pl.BlockSpec((GDN_CONV_WIDTH, gw), lambda b, t: (0, j))

    vec = pl.BlockSpec((1, LANES), lambda b, t: (0, 0))
    tok = pl.BlockSpec((None, tb, gw), lambda b, t: (b, t, 0))
    act = lambda dt: jax.ShapeDtypeStruct((batch, seq, gw), dt)
    kern = functools.partial(_gdn_prep_kernel, heads=heads, tb=tb)
    return pl.pallas_call(
        kern, grid=(batch, seq // tb),
        in_specs=[col(0), col(1), col(2),
                  pl.BlockSpec((None, tb, LANES), lambda b, t: (b, t, 0)),
                  cw(0), cw(1), cw(2), vec, vec],
        out_specs=[tok, tok, tok, tok, tok,
                   pl.BlockSpec((None, nc * 8, gw), lambda b, t: (b, t, 0))],
        out_shape=[act(F32), act(BF16), act(BF16), act(BF16), act(BF16),
                   jax.ShapeDtypeStruct((batch, seq // GDN_CHUNK * 8, gw), F32)],
        scratch_shapes=[pltpu.VMEM((tb + 8, gw), F32)] * 3,
        compiler_params=_params("parallel", "arbitrary"),
    )(p_gdn, p_gdn, p_gdn, ba, conv_w, conv_w, conv_w, a_row, d_row)


def _gdn_scan_kernel(u_ref, w_ref, qd_ref, kt_ref, qk_ref, dec_ref, z_ref, g_ref, o_ref, s_ref,
                     *, batch, heads, tb):
    c = GDN_CHUNK

    @pl.when(pl.program_id(0) == 0)
    def _():
        s_ref[...] = jnp.zeros_like(s_ref)

    def chunk(n, carry):
        r0 = pl.multiple_of(n * c, c)
        rows = pl.ds(r0, c)
        for b in range(batch):
            for h in range(heads):
                sl = slice(h * GDN_HEAD_DIM, (h + 1) * GDN_HEAD_DIM)
                s = s_ref[b * heads + h]
                sb = s.astype(BF16)
                v_new = u_ref[b, rows, sl] - _dot(w_ref[b, rows, sl], sb)
                vb = v_new.astype(BF16)
                qk = qk_ref[b, rows, sl][:, :c]
                o = _dot(qd_ref[b, rows, sl], sb) + _dot(qk, vb)
                ktv = lax.dot_general(kt_ref[b, rows, sl], vb, (((0,), (0,)), ((), ())),
                                      preferred_element_type=F32)
                dec = dec_ref[b, pl.ds(pl.multiple_of(n * 8, 8), 8), sl][0:1, :]
                s_ref[b * heads + h] = s * dec + ktv
                on = o * lax.rsqrt(jnp.mean(o * o, axis=-1, keepdims=True) + RMS_EPS) * g_ref[...]
                z = z_ref[b, rows, sl].astype(F32)
                o_ref[b, rows, sl] = (on * _silu(z)).astype(o_ref.dtype)
        return carry

    lax.fori_loop(0, tb // c, chunk, 0)


def gdn_scan(u, w, qd, kt, qk, dec, p_gdn, norm_g, batch, seq, heads):
    gw = heads * GDN_HEAD_DIM
    tb = min(512, seq)
    nc = tb // GDN_CHUNK
    tok = pl.BlockSpec((batch, tb, gw), lambda t: (0, t, 0))
    kern = functools.partial(_gdn_scan_kernel, batch=batch, heads=heads, tb=tb)
    return pl.pallas_call(
        kern, grid=(seq // tb,),
        in_specs=[tok, tok, tok, tok, tok,
                  pl.BlockSpec((batch, nc * 8, gw), lambda t: (0, t, 0)),
                  pl.BlockSpec((batch, tb, gw), lambda t: (0, t, 3)),
                  pl.BlockSpec((1, GDN_HEAD_DIM), lambda t: (0, 0))],
        out_specs=tok,
        out_shape=jax.ShapeDtypeStruct((batch, seq, gw), BF16),
        scratch_shapes=[pltpu.VMEM((batch * heads, GDN_HEAD_DIM, GDN_HEAD_DIM), F32)],
        compiler_params=_params("arbitrary"),
    )(u, w, qd, kt, qk, dec, p_gdn, norm_g.reshape(1, GDN_HEAD_DIM))


def _conf_kernel(u_ref, w_ref, b_ref, g_ref, beta_ref, o_ref, xe_ref, *, tb, ch, rb):
    halo = 32

    @pl.when(pl.program_id(1) == 0)
    def _():
        xe_ref[0:halo, :] = jnp.zeros((halo, ch), F32)

    u = u_ref[...]
    xe_ref[halo:halo + tb, :] = u[:, :ch].astype(F32) * _sigmoid(u[:, ch:].astype(F32))
    first = halo - (CONF_KERNEL - 1)

    for r in range(tb // rb):
        r0 = r * rb
        acc = jnp.broadcast_to(b_ref[...], (rb, ch))
        for j in range(CONF_KERNEL):
            acc = acc + w_ref[j:j + 1, :] * xe_ref[pl.ds(r0 + first + j, rb), :]
        mu = jnp.mean(acc, axis=-1, keepdims=True)
        xc = acc - mu
        var = jnp.mean(xc * xc, axis=-1, keepdims=True)
        y = xc * lax.rsqrt(var + RMS_EPS) * g_ref[...] + beta_ref[...]
        o_ref[pl.ds(r0, rb), :] = _silu(y).astype(o_ref.dtype)
    xe_ref[0:halo, :] = xe_ref[tb:tb + halo, :]


def conformer_conv(p_conf, dw_w, dw_b, ln_g, ln_b, batch, seq):
    ch = p_conf.shape[-1] // 2
    tb = min(512, seq)
    kern = functools.partial(_conf_kernel, tb=tb, ch=ch, rb=32)
    w_pad = jnp.zeros((32, ch), F32).at[:CONF_KERNEL].set(dw_w)
    vec = pl.BlockSpec((1, ch), lambda b, t: (0, 0))
    return pl.pallas_call(
        kern, grid=(batch, seq // tb),
        in_specs=[pl.BlockSpec((None, tb, 2 * ch), lambda b, t: (b, t, 0)),
                  pl.BlockSpec((32, ch), lambda b, t: (0, 0)), vec, vec, vec],
        out_specs=pl.BlockSpec((None, tb, ch), lambda b, t: (b, t, 0)),
        out_shape=jax.ShapeDtypeStruct((batch, seq, ch), BF16),
        scratch_shapes=[pltpu.VMEM((tb + 32, ch), F32)],
        compiler_params=_params("parallel", "arbitrary"),
    )(p_conf, w_pad, dw_b.reshape(1, ch), ln_g.reshape(1, ch), ln_b.reshape(1, ch))


def _diff_prep_kernel(q_ref, k_ref, ones_ref, gq_ref, gk_ref, qo_ref, ko_ref):
    def norm(x_ref, g_ref, o_ref):
        x = x_ref[...].astype(F32)
        ms = _dot((x * x).astype(BF16), ones_ref[...]) * (1.0 / DIFF_QK_DIM)
        o_ref[...] = (x * lax.rsqrt(ms + RMS_EPS) * g_ref[...]).astype(o_ref.dtype)

    norm(q_ref, gq_ref, qo_ref)
    norm(k_ref, gk_ref, ko_ref)


def diff_prepare(p_diff, q_gain, k_gain, heads):
    n = p_diff.shape[0]
    dw = heads * 2 * DIFF_QK_DIM
    tm = min(512, n)
    grp = jnp.arange(dw) // DIFF_QK_DIM
    ones = (grp[:, None] == grp[None, :]).astype(BF16)
    gq = (jnp.tile(q_gain, 2 * heads) * (DIFF_QK_DIM ** -0.5 * LOG2E)).reshape(1, dw)
    gk = jnp.tile(k_gain, 2 * heads).reshape(1, dw)
    vec = pl.BlockSpec((1, dw), lambda i: (0, 0))
    out = pl.BlockSpec((tm, dw), lambda i: (i, 0))
    return pl.pallas_call(
        _diff_prep_kernel, grid=(n // tm,),
        in_specs=[pl.BlockSpec((tm, dw), lambda i: (i, 0)),
                  pl.BlockSpec((tm, dw), lambda i: (i, 1)),
                  pl.BlockSpec((dw, dw), lambda i: (0, 0)), vec, vec],
        out_specs=[out, out],
        out_shape=[jax.ShapeDtypeStruct((n, dw), BF16)] * 2,
        compiler_params=_params("parallel"),
    )(p_diff, p_diff, ones, gq, gk)


ATTN_HEADS_PER_STEP = 2
ATTN_ROW_BLOCKS = 8


def _diff_attn_kernel(lam_ref, q_ref, k_ref, v_ref, sg_ref, o_ref, loc_ref, m_ref, acc_ref,
                      *, tq, heads, hps, lam_init):
    i = pl.program_id(2)
    d = DIFF_QK_DIM
    dv = DIFF_V_DIM
    lv = lam_ref[...]
    lam = (jnp.exp(jnp.sum(lv[0:1, :] * lv[1:2, :], axis=-1, keepdims=True))
           - jnp.exp(jnp.sum(lv[2:3, :] * lv[3:4, :], axis=-1, keepdims=True)) + lam_init)
    lane = lax.broadcasted_iota(jnp.int32, (tq, LANES), 1)
    rr = lax.broadcasted_iota(jnp.int32, (tq, tq), 0)
    cc = lax.broadcasted_iota(jnp.int32, (tq, tq), 1)
    ones_col = jnp.where(lane == 0, 1.0, 0.0).astype(BF16)
    m_ref[...] = jnp.full(m_ref.shape, -jnp.inf, F32)
    acc_ref[...] = jnp.zeros(acc_ref.shape, F32)

    slopes = []
    q2s = []
    for hh in range(hps):
        sl = slice(hh * LANES, (hh + 1) * LANES)
        h = pl.program_id(1) * hps + hh
        slope = jnp.exp2(jnp.full((1, 1), h + 1, jnp.int32).astype(F32) * (-8.0 / heads)) * LOG2E
        slopes.append(slope)
        q = q_ref[:, sl]
        zero = jnp.zeros_like(q)
        q2s.append(jnp.concatenate([jnp.where(lane < d, q, zero), jnp.where(lane >= d, q, zero)], axis=0))

        @pl.when(i == 0)
        def _(hh=hh, slope=slope):
            local = (cc - rr).astype(F32) * slope
            loc_ref[hh] = jnp.concatenate([local, local], axis=0)

    def step(j, masked):
        start = pl.multiple_of(j * tq, tq)
        rs = 2 * tq // ATTN_ROW_BLOCKS
        for hh in range(hps):
            sl = slice(hh * LANES, (hh + 1) * LANES)
            kj = k_ref[pl.ds(start, tq), sl]
            v2 = jnp.concatenate([v_ref[pl.ds(start, tq), sl], ones_col], axis=1)
            off = slopes[hh] * ((j - i) * tq).astype(F32)
            for r in range(ATTN_ROW_BLOCKS):
                rows = slice(r * rs, (r + 1) * rs)
                s = _dot_nt(q2s[hh][rows], kj) + loc_ref[hh, rows, :]
                if masked:
                    causal = rr >= cc
                    s = jnp.where(jnp.concatenate([causal, causal], axis=0)[rows], s, -jnp.inf)
                m_old = m_ref[hh, rows, :]
                m_new = jnp.maximum(m_old, jnp.max(s, axis=-1, keepdims=True) + off)
                mm = m_new - off
                p = jnp.concatenate([jnp.exp2(s[:, c * LANES:(c + 1) * LANES] - mm)
                                     for c in range(tq // LANES)], axis=1).astype(BF16)
                alpha = jnp.exp2(m_old - m_new)
                acc_old = acc_ref[hh, rows, :]
                acc_ref[hh, rows, :] = (jnp.concatenate([acc_old[:, :dv] * alpha, acc_old[:, dv:] * alpha], axis=1)
                                        + _dot(p, v2))
                m_ref[hh, rows, :] = m_new

    def body(j, carry):
        step(j, False)
        return carry

    lax.fori_loop(0, i, body, 0)
    step(i, True)
    for hh in range(hps):
        acc = acc_ref[hh]
        o = acc[:, :dv] / acc[:, dv:dv + 1]
        out = o[:tq] - lam * o[tq:]
        out = out * lax.rsqrt(jnp.mean(out * out, axis=-1, keepdims=True) + RMS_EPS) * sg_ref[...]
        o_ref[:, hh * LANES:(hh + 1) * LANES] = (out * (1.0 - lam_init)).astype(o_ref.dtype)


def diff_attention(qn, kn, p_diff, lam_vecs, sub_g, batch, seq, heads, lam_init):
    dw = heads * DIFF_V_DIM
    tq = min(512, seq)
    hps = ATTN_HEADS_PER_STEP
    wb = hps * LANES
    lam_pad = jnp.zeros((8, LANES), F32).at[:4, :DIFF_QK_DIM].set(lam_vecs)
    kern = functools.partial(_diff_attn_kernel, tq=tq, heads=heads, hps=hps, lam_init=lam_init)
    return pl.pallas_call(
        kern, grid=(batch, heads // hps, seq // tq),
        in_specs=[pl.BlockSpec((8, LANES), lambda b, h, i: (0, 0)),
                  pl.BlockSpec((None, tq, wb), lambda b, h, i: (b, i, h)),
                  pl.BlockSpec((None, seq, wb), lambda b, h, i: (b, 0, h)),
                  pl.BlockSpec((None, seq, wb), lambda b, h, i: (b, 0, 2 * (heads // hps) + h)),
                  pl.BlockSpec((1, DIFF_V_DIM), lambda b, h, i: (0, 0))],
        out_specs=pl.BlockSpec((None, tq, wb), lambda b, h, i: (b, i, h)),
        out_shape=jax.ShapeDtypeStruct((batch, seq, dw), BF16),
        scratch_shapes=[pltpu.VMEM((hps, 2 * tq, tq), F32), pltpu.VMEM((hps, 2 * tq, LANES), F32),
                        pltpu.VMEM((hps, 2 * tq, 2 * DIFF_V_DIM), F32)],
        compiler_params=_params("parallel", "parallel", "arbitrary"),
    )(lam_pad, qn, kn, p_diff, sub_g.reshape(1, DIFF_V_DIM))


def _merge_kernel(oa_ref, ob_ref, oc_ref, wa_ref, wb_ref, wc_ref, ga_ref, gb_ref, gc_ref, o_ref):
    mixed = _sigmoid(ga_ref[...].astype(F32)) * _dot(oa_ref[...], wa_ref[...])
    mixed = mixed + _sigmoid(gb_ref[...].astype(F32)) * _dot(ob_ref[...], wb_ref[...])
    mixed = mixed + _sigmoid(gc_ref[...].astype(F32)) * _dot(oc_ref[...], wc_ref[...])
    o_ref[...] = mixed.astype(o_ref.dtype)


def merge_branches(oa, ob, oc, wa, wb, wc, gates, d):
    n = oa.shape[0]
    tm = min(512, n)
    tn = min(512, d)
    nj = d // tn

    def lhs(x):
        return pl.BlockSpec((tm, x.shape[1]), lambda i, j: (i, 0))

    def rhs(w):
        return pl.BlockSpec((w.shape[0], tn), lambda i, j: (0, j))

    def gate(k):
        return pl.BlockSpec((tm, tn), lambda i, j: (i, k * nj + j))

    return pl.pallas_call(
        _merge_kernel, grid=(n // tm, nj),
        in_specs=[lhs(oa), lhs(ob), lhs(oc), rhs(wa), rhs(wb), rhs(wc), gate(0), gate(1), gate(2)],
        out_specs=pl.BlockSpec((tm, tn), lambda i, j: (i, j)),
        out_shape=jax.ShapeDtypeStruct((n, d), BF16),
        compiler_params=_params("parallel", "parallel"),
    )(oa, ob, oc, wa, wb, wc, gates, gates, gates)


def _proj_resid_kernel(a_ref, w_ref, x_ref, gate_ref, o_ref):
    o_ref[...] = x_ref[...] + gate_ref[...] * _dot(a_ref[...], w_ref[...])


def project_residual(a, w, x, mod4, gate_chunk, seq):
    n, k = a.shape
    d = w.shape[1]
    tm = min(1024, seq)
    tn = min(512, d)
    tpb = seq // tm
    return pl.pallas_call(
        _proj_resid_kernel, grid=(n // tm, d // tn),
        in_specs=[pl.BlockSpec((tm, k), lambda i, j: (i, 0)),
                  pl.BlockSpec((k, tn), lambda i, j: (0, j)),
                  pl.BlockSpec((tm, tn), lambda i, j: (i, j)),
                  pl.BlockSpec((None, None, 1, tn), lambda i, j: (i // tpb, gate_chunk, 0, j))],
        out_specs=pl.BlockSpec((tm, tn), lambda i, j: (i, j)),
        out_shape=jax.ShapeDtypeStruct((n, d), F32),
        compiler_params=_params("parallel", "parallel"),
    )(a, w, x, mod4)


def _route_rows(sig, sel):
    per_group = N_EXPERTS // N_GROUPS
    neg = jnp.full_like(sel[0], -jnp.inf)
    group_scores = []
    for g in range(N_GROUPS):
        r = sel[g * per_group:(g + 1) * per_group]
        best = None
        for a in range(per_group):
            for b in range(a + 1, per_group):
                pair = r[a] + r[b]
                best = pair if best is None else jnp.maximum(best, pair)
        group_scores.append(best)
    best_g = jnp.zeros_like(sel[0], dtype=jnp.int32)
    best_s = group_scores[0]
    for g in range(1, N_GROUPS):
        upd = group_scores[g] > best_s
        best_g = jnp.where(upd, g, best_g)
        best_s = jnp.where(upd, group_scores[g], best_s)
    masked = [jnp.where(best_g == (e // per_group), sel[e], neg) for e in range(N_EXPERTS)]

    def argmax_rows(rows):
        idx = jnp.zeros_like(best_g)
        val = rows[0]
        for e in range(1, N_EXPERTS):
            upd = rows[e] > val
            idx = jnp.where(upd, e, idx)
            val = jnp.where(upd, rows[e], val)
        return idx

    idx1 = argmax_rows(masked)
    idx2 = argmax_rows([jnp.where(idx1 == e, neg, masked[e]) for e in range(N_EXPERTS)])
    zero = jnp.zeros_like(sel[0])
    w1 = zero
    w2 = zero
    for e in range(N_EXPERTS):
        w1 = w1 + jnp.where(idx1 == e, sig[e], zero)
        w2 = w2 + jnp.where(idx2 == e, sig[e], zero)
    inv = 1.0 / (w1 + w2)
    return idx1, idx2, w1 * inv, w2 * inv


def _norm_route_kernel(x_ref, g_ref, sc_ref, sh_ref, rw_ref, rb_ref, tri_ref,
                       h_ref, ids_ref, wts_ref, cum_ref, carry_ref):
    @pl.when(pl.program_id(0) == 0)
    def _():
        carry_ref[...] = jnp.zeros_like(carry_ref)

    h = _norm_mod(x_ref[...], g_ref[...], sc_ref[...], sh_ref[...])
    h_ref[...] = _pack_bf16_pair(h)
    logits = lax.dot_general(rw_ref[...], h, (((1,), (1,)), ((), ())), precision=HIGHEST,
                             preferred_element_type=F32)
    sig_all = _sigmoid(logits)
    sel_all = sig_all + rb_ref[...]
    sig = [sig_all[e:e + 1, :] for e in range(N_EXPERTS)]
    sel = [sel_all[e:e + 1, :] for e in range(N_EXPERTS)]
    idx1, idx2, w1, w2 = _route_rows(sig, sel)
    onehot = jnp.concatenate(
        [jnp.where((idx1 == e) | (idx2 == e), 1.0, 0.0) for e in range(N_EXPERTS)], axis=0)
    cum = _dot(onehot.astype(BF16), tri_ref[...]) + carry_ref[...]
    tm = cum.shape[1]
    carry_ref[...] = cum[:, tm - 1:tm]
    cum_ref[...] = cum
    zero = jnp.zeros_like(w1)
    rank1 = zero
    rank2 = zero
    for e in range(N_EXPERTS):
        rank1 = rank1 + jnp.where(idx1 == e, cum[e:e + 1, :], zero)
        rank2 = rank2 + jnp.where(idx2 == e, cum[e:e + 1, :], zero)
    izero = jnp.zeros_like(idx1)
    ids_ref[...] = jnp.concatenate([idx1, idx2, (rank1 - 1.0).astype(jnp.int32),
                                    (rank2 - 1.0).astype(jnp.int32)] + [izero] * 4, axis=0)
    wts_ref[...] = jnp.concatenate([w1, w2] + [zero] * 6, axis=0)


def norm_route(x, g, mod4, scale_chunk, shift_chunk, router_w, router_bias, seq):
    n, d = x.shape
    tm = min(256, seq)
    tpb = seq // tm
    row = pl.BlockSpec((tm, d), lambda i: (i, 0))
    packed = pl.BlockSpec((tm, d // 2), lambda i: (i, 0))
    vec = pl.BlockSpec((1, d), lambda i: (0, 0))
    tri = (jnp.arange(tm)[:, None] <= jnp.arange(tm)[None, :]).astype(BF16)
    info = pl.BlockSpec((8, tm), lambda i: (0, i))
    return pl.pallas_call(
        _norm_route_kernel, grid=(n // tm,),
        in_specs=[row, vec, _mod_spec(d, tpb, scale_chunk), _mod_spec(d, tpb, shift_chunk),
                  pl.BlockSpec((N_EXPERTS, d), lambda i: (0, 0)),
                  pl.BlockSpec((N_EXPERTS, 1), lambda i: (0, 0)),
                  pl.BlockSpec((tm, tm), lambda i: (0, 0))],
        out_specs=[packed, info, info, pl.BlockSpec((N_EXPERTS, tm), lambda i: (0, i))],
        out_shape=[jax.ShapeDtypeStruct((n, d // 2), jnp.uint32), jax.ShapeDtypeStruct((8, n), jnp.int32),
                   jax.ShapeDtypeStruct((8, n), F32), jax.ShapeDtypeStruct((N_EXPERTS, n), F32)],
        scratch_shapes=[pltpu.VMEM((N_EXPERTS, 1), F32)],
        compiler_params=_params("arbitrary"),
    )(x, g.reshape(1, d), mod4, mod4, router_w.T, router_bias.reshape(N_EXPERTS, 1), tri)


MOE_FFN_TILE = 512
MOE_COMBINE_TILE = 256


def _pack_bf16_pair(x):
    k = x.shape[1] // 2
    lo = lax.bitcast_convert_type(x[:, :k].astype(BF16).astype(F32), jnp.uint32)
    hi = lax.bitcast_convert_type(x[:, k:].astype(BF16).astype(F32), jnp.uint32)
    return (hi & jnp.uint32(0xFFFF0000)) | (lo >> 16)


def _unpack_bf16_pair(w):
    lo = lax.bitcast_convert_type(w << 16, F32)
    hi = lax.bitcast_convert_type(w & jnp.uint32(0xFFFF0000), F32)
    return jnp.concatenate([lo, hi], axis=1).astype(BF16)


def _count_le(sorted_vals, x):
    return jnp.sum(sorted_vals[None, :] <= x[:, None], axis=1).astype(jnp.int32)


def moe_plan(ids, cum, n):
    tf = MOE_FFN_TILE
    e1, e2, r1, r2 = ids[0], ids[1], ids[2], ids[3]
    cnt = cum[:, n - 1].astype(jnp.int32)
    n_ft = 2 * n // tf + N_EXPERTS
    ft_per_e = (cnt + tf - 1) // tf
    ft_end = jnp.cumsum(ft_per_e)
    row_start = (ft_end - ft_per_e) * tf
    used_ft = ft_end[-1]
    d1 = row_start[e1] + r1
    d2 = row_start[e2] + r2
    ft_exp = jnp.clip(_count_le(ft_end, jnp.arange(n_ft, dtype=jnp.int32)), 0, N_EXPERTS - 1)
    n_rows = n_ft * tf
    tok = jnp.arange(n, dtype=jnp.int32)
    src = jnp.zeros((n_rows,), jnp.int32).at[d1].set(tok, unique_indices=True).at[d2].set(tok, unique_indices=True)
    return dict(d1=d1, d2=d2, src=src, n_ft=n_ft, ft_exp=ft_exp, used_ft=used_ft.astype(jnp.int32).reshape(1))


def _row_copy(src_hbm, src_row, dst_ref, dst_row, sem):
    return pltpu.make_async_copy(src_hbm.at[pl.ds(src_row, 1)], dst_ref.at[pl.ds(dst_row, 1)], sem)


def _moe_gather_kernel(src_ref, used_ref, h_hbm, xs_ref, sem):
    i = pl.program_id(0)
    tm = xs_ref.shape[0]
    base = i * tm

    @pl.when(i < used_ref[0])
    def _():
        def start(r, carry):
            _row_copy(h_hbm, src_ref[base + r], xs_ref, r, sem).start()
            return carry

        def wait(r, carry):
            _row_copy(h_hbm, 0, xs_ref, r, sem).wait()
            return carry

        lax.fori_loop(0, tm, start, 0)
        lax.fori_loop(0, tm, wait, 0)

    @pl.when(i >= used_ref[0])
    def _():
        xs_ref[...] = jnp.zeros_like(xs_ref)


def moe_gather(h2p, plan):
    d2 = h2p.shape[1]
    tm = MOE_FFN_TILE
    n_ft = plan['n_ft']
    grid_spec = pltpu.PrefetchScalarGridSpec(
        num_scalar_prefetch=2, grid=(n_ft,),
        in_specs=[pl.BlockSpec(memory_space=pl.ANY)],
        out_specs=pl.BlockSpec((tm, d2), lambda i, s, u: (i, 0)),
        scratch_shapes=[pltpu.SemaphoreType.DMA(())])
    return pl.pallas_call(
        _moe_gather_kernel, grid_spec=grid_spec,
        out_shape=jax.ShapeDtypeStruct((n_ft * tm, d2), jnp.uint32),
        compiler_params=_params("arbitrary"),
    )(plan['src'], plan['used_ft'], h2p)


def _moe_ffn_kernel(exp_ref, used_ref, x_ref, wg_ref, wu_ref, wd_ref, ws_ref, o_ref, xb_ref, acc_ref,
                    *, n_f):
    i = pl.program_id(0)
    f = pl.program_id(1)
    used = i < used_ref[0]

    @pl.when(f == 0)
    def _():
        acc_ref[...] = jnp.zeros_like(acc_ref)
        xb_ref[...] = _unpack_bf16_pair(x_ref[...])

    @pl.when(used)
    def _():
        x = xb_ref[...]
        hid = _silu(_dot(x, wg_ref[...])) * _dot(x, wu_ref[...])
        acc_ref[...] += _dot((hid * ws_ref[...]).astype(BF16), wd_ref[...])

    @pl.when(f == n_f - 1)
    def _():
        o_ref[...] = _pack_bf16_pair(acc_ref[...])


def moe_ffn(xs, ws, wg, wu, wd, plan):
    n_rows, d2 = xs.shape
    d = 2 * d2
    ff = wg.shape[-1]
    tm = MOE_FFN_TILE
    tf = min(256, ff)
    n_f = ff // tf
    kern = functools.partial(_moe_ffn_kernel, n_f=n_f)

    def row(i, f, e, u):
        return (jnp.minimum(i, u[0] - 1), 0)

    grid_spec = pltpu.PrefetchScalarGridSpec(
        num_scalar_prefetch=2, grid=(plan['n_ft'], n_f),
        in_specs=[pl.BlockSpec((tm, d2), row),
                  pl.BlockSpec((None, d, tf), lambda i, f, e, u: (e[i], 0, f)),
                  pl.BlockSpec((None, d, tf), lambda i, f, e, u: (e[i], 0, f)),
                  pl.BlockSpec((None, tf, d), lambda i, f, e, u: (e[i], f, 0)),
                  pl.BlockSpec((tm, 1), row)],
        out_specs=pl.BlockSpec((tm, d2), lambda i, f, e, u: (i, 0)),
        scratch_shapes=[pltpu.VMEM((tm, d), BF16), pltpu.VMEM((tm, d), F32)])
    return pl.pallas_call(
        kern, grid_spec=grid_spec,
        out_shape=jax.ShapeDtypeStruct((n_rows, d2), jnp.uint32),
        compiler_params=_params("arbitrary", "arbitrary"),
    )(plan['ft_exp'], plan['used_ft'], xs, wg, wu, wd, ws)


def _moe_combine_kernel(d1_ref, d2_ref, ys_hbm, y_ref, buf_ref, sem):
    i = pl.program_id(0)
    tt = y_ref.shape[0]
    base = i * tt

    def start(r, carry):
        _row_copy(ys_hbm, d1_ref[base + r], buf_ref.at[0], r, sem).start()
        _row_copy(ys_hbm, d2_ref[base + r], buf_ref.at[1], r, sem).start()
        return carry

    def wait(r, carry):
        _row_copy(ys_hbm, 0, buf_ref.at[0], r, sem).wait()
        _row_copy(ys_hbm, 0, buf_ref.at[1], r, sem).wait()
        return carry

    lax.fori_loop(0, tt, start, 0)
    lax.fori_loop(0, tt, wait, 0)
    a = buf_ref[0]
    b = buf_ref[1]
    top = jnp.uint32(0xFFFF0000)
    lo = lax.bitcast_convert_type(a << 16, F32) + lax.bitcast_convert_type(b << 16, F32)
    hi = lax.bitcast_convert_type(a & top, F32) + lax.bitcast_convert_type(b & top, F32)
    y_ref[...] = jnp.concatenate([lo, hi], axis=1).astype(y_ref.dtype)


def moe_combine(ys, plan, n):
    d2 = ys.shape[1]
    tt = min(MOE_COMBINE_TILE, n)
    grid_spec = pltpu.PrefetchScalarGridSpec(
        num_scalar_prefetch=2, grid=(n // tt,),
        in_specs=[pl.BlockSpec(memory_space=pl.ANY)],
        out_specs=pl.BlockSpec((tt, 2 * d2), lambda i, a, b: (i, 0)),
        scratch_shapes=[pltpu.VMEM((2, tt, d2), jnp.uint32), pltpu.SemaphoreType.DMA(())])
    return pl.pallas_call(
        _moe_combine_kernel, grid_spec=grid_spec,
        out_shape=jax.ShapeDtypeStruct((n, 2 * d2), BF16),
        compiler_params=_params("arbitrary"),
    )(plan['d1'], plan['d2'], ys)


def moe_sparse(h2p, ids, wts, cum, wg, wu, wd):
    n = h2p.shape[0]
    plan = moe_plan(ids, cum, n)
    n_rows = plan['n_ft'] * MOE_FFN_TILE
    ws = (jnp.zeros((n_rows,), F32).at[plan['d1']].set(wts[0], unique_indices=True)
          .at[plan['d2']].set(wts[1], unique_indices=True)).reshape(n_rows, 1)
    xs = moe_gather(h2p, plan)
    ys = moe_ffn(xs, ws, wg, wu, wd, plan)
    return moe_combine(ys, plan, n)


def kernel(x, c, ada_w, ada_b, norm1_g, w_in, gdn_conv_w, gdn_a_log, gdn_dt_bias, gdn_norm_g, gdn_w_out, conf_dw_w, conf_dw_b, conf_ln_g, conf_ln_b, conf_w_out, diff_q_norm_g, diff_k_norm_g, diff_lambda_q1, diff_lambda_k1, diff_lambda_q2, diff_lambda_k2, diff_sub_g, diff_w_out, w_o, norm2_g, router_w, router_bias, exp_w_gate, exp_w_up, exp_w_down):
    batch, seq, d = x.shape
    n = batch * seq
    depth = ada_w.shape[0]
    heads = d // D_PER_HEAD
    gw = heads * GDN_HEAD_DIM
    dw = heads * DIFF_V_DIM
    conf_ch = conf_dw_w.shape[-1]
    o_ba = 4 * gw
    o_conf = o_ba + 2 * heads
    o_diff = o_conf + 2 * conf_ch
    o_gate = o_diff + 3 * dw

    mod = ada_modulation(c, ada_w, ada_b)
    xf = x.reshape(n, d)
    y_prev = None
    mod4_prev = None
    for l in range(depth):
        mod4 = mod[l].reshape(batch, ADA_CHUNKS, 1, d)
        if y_prev is None:
            h = norm_modulate(xf, norm1_g[l], mod4, 1, 0, seq)
        else:
            xf, h = norm_modulate(xf, norm1_g[l], mod4, 1, 0, seq,
                                  y=y_prev, gate_mod4=mod4_prev, gate_chunk=5)
        wl = w_in[l]
        w_ba = jnp.zeros((d, LANES), BF16).at[:, :2 * heads].set(wl[:, o_ba:o_conf].astype(BF16))
        p_gdn = matmul(h, wl[:, :o_ba].astype(BF16), BF16).reshape(batch, seq, 4 * gw)
        ba = matmul(h, w_ba, F32).reshape(batch, seq, LANES)
        p_conf = matmul(h, wl[:, o_conf:o_diff].astype(BF16), BF16).reshape(batch, seq, 2 * conf_ch)
        p_diff = matmul(h, wl[:, o_diff:o_gate].astype(BF16), BF16)
        gates = matmul(h, wl[:, o_gate:].astype(BF16), BF16)

        u, w, qd, kt, qk, dec = gdn_prepare(p_gdn, ba, gdn_conv_w[l], gdn_a_log[l], gdn_dt_bias[l],
                                            batch, seq, heads)
        o_a = gdn_scan(u, w, qd, kt, qk, dec, p_gdn, gdn_norm_g[l], batch, seq, heads)
        o_b = conformer_conv(p_conf, conf_dw_w[l], conf_dw_b[l], conf_ln_g[l], conf_ln_b[l], batch, seq)
        qn, kn = diff_prepare(p_diff, diff_q_norm_g[l], diff_k_norm_g[l], heads)
        lam_init = 0.8 - 0.6 * math.exp(-0.3 * l)
        lam_vecs = jnp.stack([diff_lambda_q1[l], diff_lambda_k1[l], diff_lambda_q2[l], diff_lambda_k2[l]])
        o_c = diff_attention(qn.reshape(batch, seq, dw), kn.reshape(batch, seq, dw),
                             p_diff.reshape(batch, seq, 3 * dw), lam_vecs, diff_sub_g[l],
                             batch, seq, heads, lam_init)
        mixed = merge_branches(o_a.reshape(n, gw), o_b.reshape(n, conf_ch), o_c.reshape(n, dw),
                               gdn_w_out[l].astype(BF16), conf_w_out[l].astype(BF16),
                               diff_w_out[l].astype(BF16), gates, d)
        xf = project_residual(mixed, w_o[l].astype(BF16), xf, mod4, 2, seq)

        h2, ids, wts, cum = norm_route(xf, norm2_g[l], mod4, 4, 3, router_w, router_bias, seq)
        y_prev = moe_sparse(h2, ids, wts, cum, exp_w_gate[l].astype(BF16), exp_w_up[l].astype(BF16),
                            exp_w_down[l].astype(BF16))
        mod4_prev = mod4
    xf = gated_residual(xf, y_prev, mod4_prev, 5, seq)
    return xf.reshape(batch, seq, d)
```

```python
import functools
import math

import jax
import jax.numpy as jnp
from jax import lax
from jax.experimental import pallas as pl
from jax.experimental.pallas import tpu as pltpu

F32 = jnp.float32
BF16 = jnp.bfloat16
HIGHEST = lax.Precision.HIGHEST
LOG2E = math.log2(math.e)

RMS_EPS = 1e-6
LANES = 128
GDN_HEAD_DIM = 128
GDN_CONV_WIDTH = 4
GDN_CHUNK = 64
CONF_KERNEL = 31
DIFF_QK_DIM = 64
DIFF_V_DIM = 128
N_EXPERTS = 16
N_GROUPS = 4
ADA_CHUNKS = 6
D_PER_HEAD = 512
VMEM_LIMIT_BYTES = 56 * 1024 * 1024


def _params(*semantics):
    return pltpu.CompilerParams(dimension_semantics=semantics,
                                vmem_limit_bytes=VMEM_LIMIT_BYTES)


def _sigmoid(x):
    return 1.0 / (1.0 + jnp.exp(-x))


def _silu(x):
    return x * _sigmoid(x)


def _softplus(x):
    return jnp.maximum(x, 0.0) + jnp.log(1.0 + jnp.exp(-jnp.abs(x)))


def _dot(a, b):
    return jnp.dot(a, b, preferred_element_type=F32)


def _dot_nt(a, b):
    return lax.dot_general(a, b, (((1,), (1,)), ((), ())), preferred_element_type=F32)


def _ada_kernel(c_ref, w_ref, b_ref, o_ref):
    cond = _silu(c_ref[...]).astype(BF16)
    o_ref[...] = _dot(cond, w_ref[...].astype(BF16)) + b_ref[...]


def ada_modulation(c, ada_w, ada_b):
    n_layers, d, d6 = ada_w.shape
    b = c.shape[0]
    rows = 8
    c_pad = jnp.zeros((rows, d), F32).at[:b].set(c)
    tn = min(512, d6)
    out = pl.pallas_call(
        _ada_kernel,
        grid=(n_layers, d6 // tn),
        in_specs=[pl.BlockSpec((rows, d), lambda l, j: (0, 0)),
                  pl.BlockSpec((None, d, tn), lambda l, j: (l, 0, j)),
                  pl.BlockSpec((None, 1, tn), lambda l, j: (l, 0, j))],
        out_specs=pl.BlockSpec((None, rows, tn), lambda l, j: (l, 0, j)),
        out_shape=jax.ShapeDtypeStruct((n_layers, rows, d6), F32),
        compiler_params=_params("parallel", "parallel"),
    )(c_pad, ada_w, ada_b.reshape(n_layers, 1, d6))
    return out[:, :b]


def _norm_mod(x, g, scale, shift):
    ms = jnp.mean(x * x, axis=-1, keepdims=True)
    y = x * lax.rsqrt(ms + RMS_EPS) * g
    return y * (1.0 + scale) + shift


def _normmod_kernel(x_ref, g_ref, sc_ref, sh_ref, h_ref):
    h_ref[...] = _norm_mod(x_ref[...], g_ref[...], sc_ref[...], sh_ref[...]).astype(h_ref.dtype)


def _resid_normmod_kernel(x_ref, y_ref, gate_ref, g_ref, sc_ref, sh_ref, xo_ref, h_ref):
    x = x_ref[...] + gate_ref[...] * y_ref[...].astype(F32)
    xo_ref[...] = x
    h_ref[...] = _norm_mod(x, g_ref[...], sc_ref[...], sh_ref[...]).astype(h_ref.dtype)


def _resid_kernel(x_ref, y_ref, gate_ref, xo_ref):
    xo_ref[...] = x_ref[...] + gate_ref[...] * y_ref[...].astype(F32)


def _mod_spec(d, tiles_per_batch, chunk):
    return pl.BlockSpec((None, None, 1, d), lambda i: (i // tiles_per_batch, chunk, 0, 0))


def norm_modulate(x, g, mod4, scale_chunk, shift_chunk, seq, y=None, gate_mod4=None, gate_chunk=None):
    n, d = x.shape
    tm = min(256, seq)
    tpb = seq // tm
    row = pl.BlockSpec((tm, d), lambda i: (i, 0))
    vec = pl.BlockSpec((1, d), lambda i: (0, 0))
    h_shape = jax.ShapeDtypeStruct((n, d), BF16)
    if y is None:
        return pl.pallas_call(
            _normmod_kernel, grid=(n // tm,),
            in_specs=[row, vec, _mod_spec(d, tpb, scale_chunk), _mod_spec(d, tpb, shift_chunk)],
            out_specs=row, out_shape=h_shape, compiler_params=_params("parallel"),
        )(x, g.reshape(1, d), mod4, mod4)
    return pl.pallas_call(
        _resid_normmod_kernel, grid=(n // tm,),
        in_specs=[row, row, _mod_spec(d, tpb, gate_chunk), vec,
                  _mod_spec(d, tpb, scale_chunk), _mod_spec(d, tpb, shift_chunk)],
        out_specs=[row, row],
        out_shape=[jax.ShapeDtypeStruct((n, d), F32), h_shape],
        compiler_params=_params("parallel"),
    )(x, y, gate_mod4, g.reshape(1, d), mod4, mod4)


def gated_residual(x, y, mod4, gate_chunk, seq):
    n, d = x.shape
    tm = min(256, seq)
    tpb = seq // tm
    row = pl.BlockSpec((tm, d), lambda i: (i, 0))
    return pl.pallas_call(
        _resid_kernel, grid=(n // tm,),
        in_specs=[row, row, _mod_spec(d, tpb, gate_chunk)],
        out_specs=row, out_shape=jax.ShapeDtypeStruct((n, d), F32),
        compiler_params=_params("parallel"),
    )(x, y, mod4)


def _mm_kernel(a_ref, b_ref, o_ref):
    o_ref[...] = _dot(a_ref[...], b_ref[...]).astype(o_ref.dtype)


def matmul(a, b, out_dtype, tm=1024, tn=512):
    m, k = a.shape
    n = b.shape[1]
    tm = min(tm, m)
    tn = min(tn, n)
    while n % tn:
        tn -= LANES
    return pl.pallas_call(
        _mm_kernel, grid=(m // tm, n // tn),
        in_specs=[pl.BlockSpec((tm, k), lambda i, j: (i, 0)),
                  pl.BlockSpec((k, tn), lambda i, j: (0, j))],
        out_specs=pl.BlockSpec((tm, tn), lambda i, j: (i, j)),
        out_shape=jax.ShapeDtypeStruct((m, n), out_dtype),
        compiler_params=_params("parallel", "parallel"),
    )(a, b)


def _unit_lower_inverse(lower, ii, jj):
    t = jnp.broadcast_to((ii == jj).astype(F32)[None], lower.shape)
    s = 1
    while s < GDN_CHUNK:
        sh = s.bit_length() - 1
        sel = ((ii >> (sh + 1)) == (jj >> (sh + 1))) & ((ii >> sh) != (jj >> sh)) & (ii > jj)
        lo = jnp.where(sel[None], lower, 0.0)
        t_parts = _split_bf16(t, 2)
        tl = _bmm_split(t_parts, _split_bf16(lo, 2))
        t = t - _bmm_split(_split_bf16(tl, 2), t_parts)
        s *= 2
    return t


def _split_bf16(x, n):
    parts = []
    for _ in range(n - 1):
        p = x.astype(BF16)
        parts.append(p)
        x = x - p.astype(F32)
    parts.append(x.astype(BF16))
    return parts


def _bmm_split(a_parts, b_parts):
    out = None
    for ia, a in enumerate(a_parts):
        for ib, b in enumerate(b_parts):
            if ia + ib >= max(len(a_parts), len(b_parts)):
                continue
            term = jnp.einsum('nij,njk->nik', a, b, preferred_element_type=F32)
            out = term if out is None else out + term
    return out


GDN_HEADS_PER_BATCH = 2


def _gdn_prep_kernel(q_ref, k_ref, v_ref, ba_ref, cwq_ref, cwk_ref, cwv_ref, alog_ref, dt_ref,
                     u_ref, w_ref, qd_ref, kt_ref, qk_ref, dec_ref,
                     xq_ref, xk_ref, xv_ref, *, heads, tb):
    c = GDN_CHUNK
    nc = tb // c
    halo = 8

    @pl.when(pl.program_id(1) == 0)
    def _():
        for xe in (xq_ref, xk_ref, xv_ref):
            xe[0:halo, :] = jnp.zeros((halo, xe.shape[1]), F32)

    def conv_silu(x_ref, xe, cw_ref):
        xe[halo:halo + tb, :] = x_ref[...].astype(F32)
        first = halo - (GDN_CONV_WIDTH - 1)
        acc = cw_ref[0:1, :] * xe[pl.ds(first, tb), :]
        for j in range(1, GDN_CONV_WIDTH):
            acc = acc + cw_ref[j:j + 1, :] * xe[pl.ds(first + j, tb), :]
        xe[0:halo, :] = xe[tb:tb + halo, :]
        return _silu(acc)

    qa = conv_silu(q_ref, xq_ref, cwq_ref)
    ka = conv_silu(k_ref, xk_ref, cwk_ref)
    va = conv_silu(v_ref, xv_ref, cwv_ref)

    ba = ba_ref[...]
    g_all = -jnp.exp(alog_ref[...]) * _softplus(ba + dt_ref[...])
    beta_all = _sigmoid(ba).reshape(nc, c, LANES)

    ii = lax.broadcasted_iota(jnp.int32, (c, c), 0)
    jj = lax.broadcasted_iota(jnp.int32, (c, c), 1)
    tril = ii >= jj
    strict = ii > jj
    eye = ii == jj
    gc_all = _bmm_split([jnp.broadcast_to(tril.astype(BF16)[None], (nc, c, c))],
                        _split_bf16(g_all.reshape(nc, c, LANES), 3))

    hb = min(GDN_HEADS_PER_BATCH, heads)
    nb = hb * nc
    tril_b = jnp.broadcast_to(tril.astype(BF16)[None], (nb, c, c))
    ones_b = jnp.ones((nb, c, c), BF16)
    for h0 in range(0, heads, hb):
        hs = range(h0, h0 + hb)

        def stack(x2d):
            return jnp.concatenate(
                [x2d[:, h * GDN_HEAD_DIM:(h + 1) * GDN_HEAD_DIM].reshape(nc, c, GDN_HEAD_DIM) for h in hs], axis=0)

        q3 = stack(qa)
        k3 = stack(ka)
        v3 = stack(va)
        q3 = q3 * lax.rsqrt(jnp.sum(q3 * q3, axis=-1, keepdims=True) + RMS_EPS) * (GDN_HEAD_DIM ** -0.5)
        k3 = k3 * lax.rsqrt(jnp.sum(k3 * k3, axis=-1, keepdims=True) + RMS_EPS)
        gcol = jnp.concatenate([gc_all[:, :, heads + h:heads + h + 1] for h in hs], axis=0)
        bcol = jnp.concatenate([beta_all[:, :, h:h + 1] for h in hs], axis=0)
        gcol_b = jnp.broadcast_to(gcol, (nb, c, c))
        grow_b = _bmm_split([ones_b], _split_bf16(jnp.where(eye[None], gcol_b, 0.0), 3))
        decay = jnp.where(tril[None], jnp.exp(jnp.where(tril[None], gcol_b - grow_b, 0.0)), 0.0)
        glast = gcol[:, c - 1:c, :]
        eg = jnp.exp(gcol)
        kb = k3 * bcol
        k3b = k3.astype(BF16)
        kk = jnp.einsum('nid,njd->nij', kb.astype(BF16), k3b, preferred_element_type=F32)
        lower = jnp.where(strict[None], kk * decay, 0.0)
        t_inv = _unit_lower_inverse(lower, ii, jj).astype(BF16)
        u = jnp.einsum('nij,njd->nid', t_inv, (v3 * bcol).astype(BF16), preferred_element_type=F32)
        w = jnp.einsum('nij,njd->nid', t_inv, (kb * eg).astype(BF16), preferred_element_type=F32)
        qk = jnp.einsum('nid,njd->nij', q3.astype(BF16), k3b, preferred_element_type=F32) * decay
        qd = q3 * eg
        kt = k3 * jnp.exp(glast - gcol)
        qk_pad = jnp.concatenate([qk, jnp.zeros_like(qk)], axis=-1)
        dec = jnp.broadcast_to(jnp.exp(glast), (nb, 8, GDN_HEAD_DIM))
        for g, h in enumerate(hs):
            sl = slice(h * GDN_HEAD_DIM, (h + 1) * GDN_HEAD_DIM)
            part = slice(g * nc, (g + 1) * nc)
            u_ref[:, sl] = u[part].reshape(tb, GDN_HEAD_DIM)
            w_ref[:, sl] = w[part].reshape(tb, GDN_HEAD_DIM).astype(BF16)
            qd_ref[:, sl] = qd[part].reshape(tb, GDN_HEAD_DIM).astype(BF16)
            kt_ref[:, sl] = kt[part].reshape(tb, GDN_HEAD_DIM).astype(BF16)
            qk_ref[:, sl] = qk_pad[part].reshape(tb, GDN_HEAD_DIM).astype(BF16)
            dec_ref[:, sl] = dec[part].reshape(nc * 8, GDN_HEAD_DIM)


def gdn_prepare(p_gdn, ba, conv_w, a_log, dt_bias, batch, seq, heads):
    gw = heads * GDN_HEAD_DIM
    tb = min(512, seq)
    nc = tb // GDN_CHUNK
    a_row = jnp.zeros((1, LANES), F32).at[0, heads:2 * heads].set(a_log)
    d_row = jnp.zeros((1, LANES), F32).at[0, heads:2 * heads].set(dt_bias)

    def col(j):
        return pl.BlockSpec((None, tb, gw), lambda b, t: (b, t, j))

    def cw(j):
        return pl.BlockSpec((GDN_CONV_WIDTH, gw), lambda b, t: (0, j))

    vec = pl.BlockSpec((1, LANES), lambda b, t: (0, 0))
    tok = pl.BlockSpec((None, tb, gw), lambda b, t: (b, t, 0))
    act = lambda dt: jax.ShapeDtypeStruct((batch, seq, gw), dt)
    kern = functools.partial(_gdn_prep_kernel, heads=heads, tb=tb)
    return pl.pallas_call(
        kern, grid=(batch, seq // tb),
        in_specs=[col(0), col(1), col(2),
                  pl.BlockSpec((None, tb, LANES), lambda b, t: (b, t, 0)),
                  cw(0), cw(1), cw(2), vec, vec],
        out_specs=[tok, tok, tok, tok, tok,
                   pl.BlockSpec((None, nc * 8, gw), lambda b, t: (b, t, 0))],
        out_shape=[act(F32), act(BF16), act(BF16), act(BF16), act(BF16),
                   jax.ShapeDtypeStruct((batch, seq // GDN_CHUNK * 8, gw), F32)],
        scratch_shapes=[pltpu.VMEM((tb + 8, gw), F32)] * 3,
        compiler_params=_params("parallel", "arbitrary"),
    )(p_gdn, p_gdn, p_gdn, ba, conv_w, conv_w, conv_w, a_row, d_row)


def _gdn_scan_kernel(u_ref, w_ref, qd_ref, kt_ref, qk_ref, dec_ref, z_ref, g_ref, o_ref, s_ref,
                     *, batch, heads, tb):
    c = GDN_CHUNK

    @pl.when(pl.program_id(0) == 0)
    def _():
        s_ref[...] = jnp.zeros_like(s_ref)

    def chunk(n, carry):
        r0 = pl.multiple_of(n * c, c)
        rows = pl.ds(r0, c)
        for b in range(batch):
            for h in range(heads):
                sl = slice(h * GDN_HEAD_DIM, (h + 1) * GDN_HEAD_DIM)
                s = s_ref[b * heads + h]
                sb = s.astype(BF16)
                v_new = u_ref[b, rows, sl] - _dot(w_ref[b, rows, sl], sb)
                vb = v_new.astype(BF16)
                qk = qk_ref[b, rows, sl][:, :c]
                o = _dot(qd_ref[b, rows, sl], sb) + _dot(qk, vb)
                ktv = lax.dot_general(kt_ref[b, rows, sl], vb, (((0,), (0,)), ((), ())),
                                      preferred_element_type=F32)
                dec = dec_ref[b, pl.ds(pl.multiple_of(n * 8, 8), 8), sl][0:1, :]
                s_ref[b * heads + h] = s * dec + ktv
                on = o * lax.rsqrt(jnp.mean(o * o, axis=-1, keepdims=True) + RMS_EPS) * g_ref[...]
                z = z_ref[b, rows, sl].astype(F32)
                o_ref[b, rows, sl] = (on * _silu(z)).astype(o_ref.dtype)
        return carry

    lax.fori_loop(0, tb // c, chunk, 0)


def gdn_scan(u, w, qd, kt, qk, dec, p_gdn, norm_g, batch, seq, heads):
    gw = heads * GDN_HEAD_DIM
    tb = min(512, seq)
    nc = tb // GDN_CHUNK
    tok = pl.BlockSpec((batch, tb, gw), lambda t: (0, t, 0))
    kern = functools.partial(_gdn_scan_kernel, batch=batch, heads=heads, tb=tb)
    return pl.pallas_call(
        kern, grid=(seq // tb,),
        in_specs=[tok, tok, tok, tok, tok,
                  pl.BlockSpec((batch, nc * 8, gw), lambda t: (0, t, 0)),
                  pl.BlockSpec((batch, tb, gw), lambda t: (0, t, 3)),
                  pl.BlockSpec((1, GDN_HEAD_DIM), lambda t: (0, 0))],
        out_specs=tok,
        out_shape=jax.ShapeDtypeStruct((batch, seq, gw), BF16),
        scratch_shapes=[pltpu.VMEM((batch * heads, GDN_HEAD_DIM, GDN_HEAD_DIM), F32)],
        compiler_params=_params("arbitrary"),
    )(u, w, qd, kt, qk, dec, p_gdn, norm_g.reshape(1, GDN_HEAD_DIM))


def _conf_kernel(u_ref, w_ref, b_ref, g_ref, beta_ref, o_ref, xe_ref, *, tb, ch, rb):
    halo = 32

    @pl.when(pl.program_id(1) == 0)
    def _():
        xe_ref[0:halo, :] = jnp.zeros((halo, ch), F32)

    u = u_ref[...]
    xe_ref[halo:halo + tb, :] = u[:, :ch].astype(F32) * _sigmoid(u[:, ch:].astype(F32))
    first = halo - (CONF_KERNEL - 1)

    for r in range(tb // rb):
        r0 = r * rb
        acc = jnp.broadcast_to(b_ref[...], (rb, ch))
        for j in range(CONF_KERNEL):
            acc = acc + w_ref[j:j + 1, :] * xe_ref[pl.ds(r0 + first + j, rb), :]
        mu = jnp.mean(acc, axis=-1, keepdims=True)
        xc = acc - mu
        var = jnp.mean(xc * xc, axis=-1, keepdims=True)
        y = xc * lax.rsqrt(var + RMS_EPS) * g_ref[...] + beta_ref[...]
        o_ref[pl.ds(r0, rb), :] = _silu(y).astype(o_ref.dtype)
    xe_ref[0:halo, :] = xe_ref[tb:tb + halo, :]


def conformer_conv(p_conf, dw_w, dw_b, ln_g, ln_b, batch, seq):
    ch = p_conf.shape[-1] // 2
    tb = min(512, seq)
    kern = functools.partial(_conf_kernel, tb=tb, ch=ch, rb=32)
    w_pad = jnp.zeros((32, ch), F32).at[:CONF_KERNEL].set(dw_w)
    vec = pl.BlockSpec((1, ch), lambda b, t: (0, 0))
    return pl.pallas_call(
        kern, grid=(batch, seq // tb),
        in_specs=[pl.BlockSpec((None, tb, 2 * ch), lambda b, t: (b, t, 0)),
                  pl.BlockSpec((32, ch), lambda b, t: (0, 0)), vec, vec, vec],
        out_specs=pl.BlockSpec((None, tb, ch), lambda b, t: (b, t, 0)),
        out_shape=jax.ShapeDtypeStruct((batch, seq, ch), BF16),
        scratch_shapes=[pltpu.VMEM((tb + 32, ch), F32)],
        compiler_params=_params("parallel", "arbitrary"),
    )(p_conf, w_pad, dw_b.reshape(1, ch), ln_g.reshape(1, ch), ln_b.reshape(1, ch))


def _diff_prep_kernel(q_ref, k_ref, ones_ref, gq_ref, gk_ref, qo_ref, ko_ref):
    def norm(x_ref, g_ref, o_ref):
        x = x_ref[...].astype(F32)
        ms = _dot((x * x).astype(BF16), ones_ref[...]) * (1.0 / DIFF_QK_DIM)
        o_ref[...] = (x * lax.rsqrt(ms + RMS_EPS) * g_ref[...]).astype(o_ref.dtype)

    norm(q_ref, gq_ref, qo_ref)
    norm(k_ref, gk_ref, ko_ref)


def diff_prepare(p_diff, q_gain, k_gain, heads):
    n = p_diff.shape[0]
    dw = heads * 2 * DIFF_QK_DIM
    tm = min(512, n)
    grp = jnp.arange(dw) // DIFF_QK_DIM
    ones = (grp[:, None] == grp[None, :]).astype(BF16)
    gq = (jnp.tile(q_gain, 2 * heads) * (DIFF_QK_DIM ** -0.5 * LOG2E)).reshape(1, dw)
    gk = jnp.tile(k_gain, 2 * heads).reshape(1, dw)
    vec = pl.BlockSpec((1, dw), lambda i: (0, 0))
    out = pl.BlockSpec((tm, dw), lambda i: (i, 0))
    return pl.pallas_call(
        _diff_prep_kernel, grid=(n // tm,),
        in_specs=[pl.BlockSpec((tm, dw), lambda i: (i, 0)),
                  pl.BlockSpec((tm, dw), lambda i: (i, 1)),
                  pl.BlockSpec((dw, dw), lambda i: (0, 0)), vec, vec],
        out_specs=[out, out],
        out_shape=[jax.ShapeDtypeStruct((n, dw), BF16)] * 2,
        compiler_params=_params("parallel"),
    )(p_diff, p_diff, ones, gq, gk)


ATTN_HEADS_PER_STEP = 2
ATTN_ROW_BLOCKS = 8


def _diff_attn_kernel(lam_ref, q_ref, k_ref, v_ref, sg_ref, o_ref, loc_ref, m_ref, acc_ref,
                      *, tq, heads, hps, lam_init):
    i = pl.program_id(2)
    d = DIFF_QK_DIM
    dv = DIFF_V_DIM
    lv = lam_ref[...]
    lam = (jnp.exp(jnp.sum(lv[0:1, :] * lv[1:2, :], axis=-1, keepdims=True))
           - jnp.exp(jnp.sum(lv[2:3, :] * lv[3:4, :], axis=-1, keepdims=True)) + lam_init)
    lane = lax.broadcasted_iota(jnp.int32, (tq, LANES), 1)
    rr = lax.broadcasted_iota(jnp.int32, (tq, tq), 0)
    cc = lax.broadcasted_iota(jnp.int32, (tq, tq), 1)
    ones_col = jnp.where(lane == 0, 1.0, 0.0).astype(BF16)
    m_ref[...] = jnp.full(m_ref.shape, -jnp.inf, F32)
    acc_ref[...] = jnp.zeros(acc_ref.shape, F32)

    slopes = []
    q2s = []
    for hh in range(hps):
        sl = slice(hh * LANES, (hh + 1) * LANES)
        h = pl.program_id(1) * hps + hh
        slope = jnp.exp2(jnp.full((1, 1), h + 1, jnp.int32).astype(F32) * (-8.0 / heads)) * LOG2E
        slopes.append(slope)
        q = q_ref[:, sl]
        zero = jnp.zeros_like(q)
        q2s.append(jnp.concatenate([jnp.where(lane < d, q, zero), jnp.where(lane >= d, q, zero)], axis=0))

        @pl.when(i == 0)
        def _(hh=hh, slope=slope):
            local = (cc - rr).astype(F32) * slope
            loc_ref[hh] = jnp.concatenate([local, local], axis=0)

    def step(j, masked):
        start = pl.multiple_of(j * tq, tq)
        rs = 2 * tq // ATTN_ROW_BLOCKS
        for hh in range(hps):
            sl = slice(hh * LANES, (hh + 1) * LANES)
            kj = k_ref[pl.ds(start, tq), sl]
            v2 = jnp.concatenate([v_ref[pl.ds(start, tq), sl], ones_col], axis=1)
            off = slopes[hh] * ((j - i) * tq).astype(F32)
            for r in range(ATTN_ROW_BLOCKS):
                rows = slice(r * rs, (r + 1) * rs)
                s = _dot_nt(q2s[hh][rows], kj) + loc_ref[hh, rows, :]
                if masked:
                    causal = rr >= cc
                    s = jnp.where(jnp.concatenate([causal, causal], axis=0)[rows], s, -jnp.inf)
                m_old = m_ref[hh, rows, :]
                m_new = jnp.maximum(m_old, jnp.max(s, axis=-1, keepdims=True) + off)
                mm = m_new - off
                p = jnp.concatenate([jnp.exp2(s[:, c * LANES:(c + 1) * LANES] - mm)
                                     for c in range(tq // LANES)], axis=1).astype(BF16)
                alpha = jnp.exp2(m_old - m_new)
                acc_old = acc_ref[hh, rows, :]
                acc_ref[hh, rows, :] = (jnp.concatenate([acc_old[:, :dv] * alpha, acc_old[:, dv:] * alpha], axis=1)
                                        + _dot(p, v2))
                m_ref[hh, rows, :] = m_new

    def body(j, carry):
        step(j, False)
        return carry

    lax.fori_loop(0, i, body, 0)
    step(i, True)
    for hh in range(hps):
        acc = acc_ref[hh]
        o = acc[:, :dv] / acc[:, dv:dv + 1]
        out = o[:tq] - lam * o[tq:]
        out = out * lax.rsqrt(jnp.mean(out * out, axis=-1, keepdims=True) + RMS_EPS) * sg_ref[...]
        o_ref[:, hh * LANES:(hh + 1) * LANES] = (out * (1.0 - lam_init)).astype(o_ref.dtype)


def diff_attention(qn, kn, p_diff, lam_vecs, sub_g, batch, seq, heads, lam_init):
    dw = heads * DIFF_V_DIM
    tq = min(512, seq)
    hps = ATTN_HEADS_PER_STEP
    wb = hps * LANES
    lam_pad = jnp.zeros((8, LANES), F32).at[:4, :DIFF_QK_DIM].set(lam_vecs)
    kern = functools.partial(_diff_attn_kernel, tq=tq, heads=heads, hps=hps, lam_init=lam_init)
    return pl.pallas_call(
        kern, grid=(batch, heads // hps, seq // tq),
        in_specs=[pl.BlockSpec((8, LANES), lambda b, h, i: (0, 0)),
                  pl.BlockSpec((None, tq, wb), lambda b, h, i: (b, i, h)),
                  pl.BlockSpec((None, seq, wb), lambda b, h, i: (b, 0, h)),
                  pl.BlockSpec((None, seq, wb), lambda b, h, i: (b, 0, 2 * (heads // hps) + h)),
                  pl.BlockSpec((1, DIFF_V_DIM), lambda b, h, i: (0, 0))],
        out_specs=pl.BlockSpec((None, tq, wb), lambda b, h, i: (b, i, h)),
        out_shape=jax.ShapeDtypeStruct((batch, seq, dw), BF16),
        scratch_shapes=[pltpu.VMEM((hps, 2 * tq, tq), F32), pltpu.VMEM((hps, 2 * tq, LANES), F32),
                        pltpu.VMEM((hps, 2 * tq, 2 * DIFF_V_DIM), F32)],
        compiler_params=_params("parallel", "parallel", "arbitrary"),
    )(lam_pad, qn, kn, p_diff, sub_g.reshape(1, DIFF_V_DIM))


def _merge_kernel(oa_ref, ob_ref, oc_ref, wa_ref, wb_ref, wc_ref, ga_ref, gb_ref, gc_ref, o_ref):
    mixed = _sigmoid(ga_ref[...].astype(F32)) * _dot(oa_ref[...], wa_ref[...])
    mixed = mixed + _sigmoid(gb_ref[...].astype(F32)) * _dot(ob_ref[...], wb_ref[...])
    mixed = mixed + _sigmoid(gc_ref[...].astype(F32)) * _dot(oc_ref[...], wc_ref[...])
    o_ref[...] = mixed.astype(o_ref.dtype)


def merge_branches(oa, ob, oc, wa, wb, wc, gates, d):
    n = oa.shape[0]
    tm = min(512, n)
    tn = min(512, d)
    nj = d // tn

    def lhs(x):
        return pl.BlockSpec((tm, x.shape[1]), lambda i, j: (i, 0))

    def rhs(w):
        return pl.BlockSpec((w.shape[0], tn), lambda i, j: (0, j))

    def gate(k):
        return pl.BlockSpec((tm, tn), lambda i, j: (i, k * nj + j))

    return pl.pallas_call(
        _merge_kernel, grid=(n // tm, nj),
        in_specs=[lhs(oa), lhs(ob), lhs(oc), rhs(wa), rhs(wb), rhs(wc), gate(0), gate(1), gate(2)],
        out_specs=pl.BlockSpec((tm, tn), lambda i, j: (i, j)),
        out_shape=jax.ShapeDtypeStruct((n, d), BF16),
        compiler_params=_params("parallel", "parallel"),
    )(oa, ob, oc, wa, wb, wc, gates, gates, gates)


def _proj_resid_kernel(a_ref, w_ref, x_ref, gate_ref, o_ref):
    o_ref[...] = x_ref[...] + gate_ref[...] * _dot(a_ref[...], w_ref[...])


def project_residual(a, w, x, mod4, gate_chunk, seq):
    n, k = a.shape
    d = w.shape[1]
    tm = min(1024, seq)
    tn = min(512, d)
    tpb = seq // tm
    return pl.pallas_call(
        _proj_resid_kernel, grid=(n // tm, d // tn),
        in_specs=[pl.BlockSpec((tm, k), lambda i, j: (i, 0)),
                  pl.BlockSpec((k, tn), lambda i, j: (0, j)),
                  pl.BlockSpec((tm, tn), lambda i, j: (i, j)),
                  pl.BlockSpec((None, None, 1, tn), lambda i, j: (i // tpb, gate_chunk, 0, j))],
        out_specs=pl.BlockSpec((tm, tn), lambda i, j: (i, j)),
        out_shape=jax.ShapeDtypeStruct((n, d), F32),
        compiler_params=_params("parallel", "parallel"),
    )(a, w, x, mod4)


def _route_rows(sig, sel):
    per_group = N_EXPERTS // N_GROUPS
    neg = jnp.full_like(sel[0], -jnp.inf)
    group_scores = []
    for g in range(N_GROUPS):
        r = sel[g * per_group:(g + 1) * per_group]
        best = None
        for a in range(per_group):
            for b in range(a + 1, per_group):
                pair = r[a] + r[b]
                best = pair if best is None else jnp.maximum(best, pair)
        group_scores.append(best)
    best_g = jnp.zeros_like(sel[0], dtype=jnp.int32)
    best_s = group_scores[0]
    for g in range(1, N_GROUPS):
        upd = group_scores[g] > best_s
        best_g = jnp.where(upd, g, best_g)
        best_s = jnp.where(upd, group_scores[g], best_s)
    masked = [jnp.where(best_g == (e // per_group), sel[e], neg) for e in range(N_EXPERTS)]

    def argmax_rows(rows):
        idx = jnp.zeros_like(best_g)
        val = rows[0]
        for e in range(1, N_EXPERTS):
            upd = rows[e] > val
            idx = jnp.where(upd, e, idx)
            val = jnp.where(upd, rows[e], val)
        return idx

    idx1 = argmax_rows(masked)
    idx2 = argmax_rows([jnp.where(idx1 == e, neg, masked[e]) for e in range(N_EXPERTS)])
    zero = jnp.zeros_like(sel[0])
    w1 = zero
    w2 = zero
    for e in range(N_EXPERTS):
        w1 = w1 + jnp.where(idx1 == e, sig[e], zero)
        w2 = w2 + jnp.where(idx2 == e, sig[e], zero)
    inv = 1.0 / (w1 + w2)
    return idx1, idx2, w1 * inv, w2 * inv


def _norm_route_kernel(x_ref, g_ref, sc_ref, sh_ref, rw_ref, rb_ref, tri_ref,
                       h_ref, ids_ref, wts_ref, cum_ref, carry_ref):
    @pl.when(pl.program_id(0) == 0)
    def _():
        carry_ref[...] = jnp.zeros_like(carry_ref)

    h = _norm_mod(x_ref[...], g_ref[...], sc_ref[...], sh_ref[...])
    h_ref[...] = _pack_bf16_pair(h)
    logits = lax.dot_general(rw_ref[...], h, (((1,), (1,)), ((), ())), precision=HIGHEST,
                             preferred_element_type=F32)
    sig_all = _sigmoid(logits)
    sel_all = sig_all + rb_ref[...]
    sig = [sig_all[e:e + 1, :] for e in range(N_EXPERTS)]
    sel = [sel_all[e:e + 1, :] for e in range(N_EXPERTS)]
    idx1, idx2, w1, w2 = _route_rows(sig, sel)
    onehot = jnp.concatenate(
        [jnp.where((idx1 == e) | (idx2 == e), 1.0, 0.0) for e in range(N_EXPERTS)], axis=0)
    cum = _dot(onehot.astype(BF16), tri_ref[...]) + carry_ref[...]
    tm = cum.shape[1]
    carry_ref[...] = cum[:, tm - 1:tm]
    cum_ref[...] = cum
    zero = jnp.zeros_like(w1)
    rank1 = zero
    rank2 = zero
    for e in range(N_EXPERTS):
        rank1 = rank1 + jnp.where(idx1 == e, cum[e:e + 1, :], zero)
        rank2 = rank2 + jnp.where(idx2 == e, cum[e:e + 1, :], zero)
    izero = jnp.zeros_like(idx1)
    ids_ref[...] = jnp.concatenate([idx1, idx2, (rank1 - 1.0).astype(jnp.int32),
                                    (rank2 - 1.0).astype(jnp.int32)] + [izero] * 4, axis=0)
    wts_ref[...] = jnp.concatenate([w1, w2] + [zero] * 6, axis=0)


def norm_route(x, g, mod4, scale_chunk, shift_chunk, router_w, router_bias, seq):
    n, d = x.shape
    tm = min(256, seq)
    tpb = seq // tm
    row = pl.BlockSpec((tm, d), lambda i: (i, 0))
    packed = pl.BlockSpec((tm, d // 2), lambda i: (i, 0))
    vec = pl.BlockSpec((1, d), lambda i: (0, 0))
    tri = (jnp.arange(tm)[:, None] <= jnp.arange(tm)[None, :]).astype(BF16)
    info = pl.BlockSpec((8, tm), lambda i: (0, i))
    return pl.pallas_call(
        _norm_route_kernel, grid=(n // tm,),
        in_specs=[row, vec, _mod_spec(d, tpb, scale_chunk), _mod_spec(d, tpb, shift_chunk),
                  pl.BlockSpec((N_EXPERTS, d), lambda i: (0, 0)),
                  pl.BlockSpec((N_EXPERTS, 1), lambda i: (0, 0)),
                  pl.BlockSpec((tm, tm), lambda i: (0, 0))],
        out_specs=[packed, info, info, pl.BlockSpec((N_EXPERTS, tm), lambda i: (0, i))],
        out_shape=[jax.ShapeDtypeStruct((n, d // 2), jnp.uint32), jax.ShapeDtypeStruct((8, n), jnp.int32),
                   jax.ShapeDtypeStruct((8, n), F32), jax.ShapeDtypeStruct((N_EXPERTS, n), F32)],
        scratch_shapes=[pltpu.VMEM((N_EXPERTS, 1), F32)],
        compiler_params=_params("arbitrary"),
    )(x, g.reshape(1, d), mod4, mod4, router_w.T, router_bias.reshape(N_EXPERTS, 1), tri)


MOE_FFN_TILE = 512
MOE_COMBINE_TILE = 256


def _pack_bf16_pair(x):
    k = x.shape[1] // 2
    lo = lax.bitcast_convert_type(x[:, :k].astype(BF16).astype(F32), jnp.uint32)
    hi = lax.bitcast_convert_type(x[:, k:].astype(BF16).astype(F32), jnp.uint32)
    return (hi & jnp.uint32(0xFFFF0000)) | (lo >> 16)


def _unpack_bf16_pair(w):
    lo = lax.bitcast_convert_type(w << 16, F32)
    hi = lax.bitcast_convert_type(w & jnp.uint32(0xFFFF0000), F32)
    return jnp.concatenate([lo, hi], axis=1).astype(BF16)


def _count_le(sorted_vals, x):
    return jnp.sum(sorted_vals[None, :] <= x[:, None], axis=1).astype(jnp.int32)


def moe_plan(ids, cum, n):
    tf = MOE_FFN_TILE
    e1, e2, r1, r2 = ids[0], ids[1], ids[2], ids[3]
    cnt = cum[:, n - 1].astype(jnp.int32)
    n_ft = 2 * n // tf + N_EXPERTS
    ft_per_e = (cnt + tf - 1) // tf
    ft_end = jnp.cumsum(ft_per_e)
    row_start = (ft_end - ft_per_e) * tf
    used_ft = ft_end[-1]
    d1 = row_start[e1] + r1
    d2 = row_start[e2] + r2
    ft_exp = jnp.clip(_count_le(ft_end, jnp.arange(n_ft, dtype=jnp.int32)), 0, N_EXPERTS - 1)
    n_rows = n_ft * tf
    tok = jnp.arange(n, dtype=jnp.int32)
    src = jnp.zeros((n_rows,), jnp.int32).at[d1].set(tok, unique_indices=True).at[d2].set(tok, unique_indices=True)
    return dict(d1=d1, d2=d2, src=src, n_ft=n_ft, ft_exp=ft_exp, used_ft=used_ft.astype(jnp.int32).reshape(1))


def _row_copy(src_hbm, src_row, dst_ref, dst_row, sem):
    return pltpu.make_async_copy(src_hbm.at[pl.ds(src_row, 1)], dst_ref.at[pl.ds(dst_row, 1)], sem)


def _moe_gather_kernel(src_ref, used_ref, h_hbm, xs_ref, sem):
    i = pl.program_id(0)
    tm = xs_ref.shape[0]
    base = i * tm

    @pl.when(i < used_ref[0])
    def _():
        def start(r, carry):
            _row_copy(h_hbm, src_ref[base + r], xs_ref, r, sem).start()
            return carry

        def wait(r, carry):
            _row_copy(h_hbm, 0, xs_ref, r, sem).wait()
            return carry

        lax.fori_loop(0, tm, start, 0)
        lax.fori_loop(0, tm, wait, 0)

    @pl.when(i >= used_ref[0])
    def _():
        xs_ref[...] = jnp.zeros_like(xs_ref)


def moe_gather(h2p, plan):
    d2 = h2p.shape[1]
    tm = MOE_FFN_TILE
    n_ft = plan['n_ft']
    grid_spec = pltpu.PrefetchScalarGridSpec(
        num_scalar_prefetch=2, grid=(n_ft,),
        in_specs=[pl.BlockSpec(memory_space=pl.ANY)],
        out_specs=pl.BlockSpec((tm, d2), lambda i, s, u: (i, 0)),
        scratch_shapes=[pltpu.SemaphoreType.DMA(())])
    return pl.pallas_call(
        _moe_gather_kernel, grid_spec=grid_spec,
        out_shape=jax.ShapeDtypeStruct((n_ft * tm, d2), jnp.uint32),
        compiler_params=_params("arbitrary"),
    )(plan['src'], plan['used_ft'], h2p)


def _moe_ffn_kernel(exp_ref, used_ref, x_ref, wg_ref, wu_ref, wd_ref, o_ref, xb_ref, acc_ref, *, n_f):
    i = pl.program_id(0)
    f = pl.program_id(1)
    used = i < used_ref[0]

    @pl.when(f == 0)
    def _():
        acc_ref[...] = jnp.zeros_like(acc_ref)
        xb_ref[...] = _unpack_bf16_pair(x_ref[...])

    @pl.when(used)
    def _():
        x = xb_ref[...]
        hid = _silu(_dot(x, wg_ref[...])) * _dot(x, wu_ref[...])
        acc_ref[...] += _dot(hid.astype(BF16), wd_ref[...])

    @pl.when(f == n_f - 1)
    def _():
        o_ref[...] = _pack_bf16_pair(acc_ref[...])


def moe_ffn(xs, wg, wu, wd, plan):
    n_rows, d2 = xs.shape
    d = 2 * d2
    ff = wg.shape[-1]
    tm = MOE_FFN_TILE
    tf = min(256, ff)
    n_f = ff // tf
    kern = functools.partial(_moe_ffn_kernel, n_f=n_f)

    def row(i, f, e, u):
        return (jnp.minimum(i, u[0] - 1), 0)

    grid_spec = pltpu.PrefetchScalarGridSpec(
        num_scalar_prefetch=2, grid=(plan['n_ft'], n_f),
        in_specs=[pl.BlockSpec((tm, d2), row),
                  pl.BlockSpec((None, d, tf), lambda i, f, e, u: (e[i], 0, f)),
                  pl.BlockSpec((None, d, tf), lambda i, f, e, u: (e[i], 0, f)),
                  pl.BlockSpec((None, tf, d), lambda i, f, e, u: (e[i], f, 0))],
        out_specs=pl.BlockSpec((tm, d2), lambda i, f, e, u: (i, 0)),
        scratch_shapes=[pltpu.VMEM((tm, d), BF16), pltpu.VMEM((tm, d), F32)])
    return pl.pallas_call(
        kern, grid_spec=grid_spec,
        out_shape=jax.ShapeDtypeStruct((n_rows, d2), jnp.uint32),
        compiler_params=_params("arbitrary", "arbitrary"),
    )(plan['ft_exp'], plan['used_ft'], xs, wg, wu, wd)


def _moe_combine_kernel(d1_ref, d2_ref, ys_hbm, wt_ref, y_ref, buf_ref, sem):
    i = pl.program_id(0)
    tt = y_ref.shape[0]
    base = i * tt

    def start(r, carry):
        _row_copy(ys_hbm, d1_ref[base + r], buf_ref.at[0], r, sem).start()
        _row_copy(ys_hbm, d2_ref[base + r], buf_ref.at[1], r, sem).start()
        return carry

    def wait(r, carry):
        _row_copy(ys_hbm, 0, buf_ref.at[0], r, sem).wait()
        _row_copy(ys_hbm, 0, buf_ref.at[1], r, sem).wait()
        return carry

    lax.fori_loop(0, tt, start, 0)
    lax.fori_loop(0, tt, wait, 0)
    a = buf_ref[0]
    b = buf_ref[1]
    top = jnp.uint32(0xFFFF0000)
    w1 = wt_ref[:, 0:1]
    w2 = wt_ref[:, 1:2]
    lo = w1 * lax.bitcast_convert_type(a << 16, F32) + w2 * lax.bitcast_convert_type(b << 16, F32)
    hi = w1 * lax.bitcast_convert_type(a & top, F32) + w2 * lax.bitcast_convert_type(b & top, F32)
    y_ref[...] = jnp.concatenate([lo, hi], axis=1).astype(y_ref.dtype)


def moe_combine(ys, wts_t, plan, n):
    d2 = ys.shape[1]
    tt = min(MOE_COMBINE_TILE, n)
    grid_spec = pltpu.PrefetchScalarGridSpec(
        num_scalar_prefetch=2, grid=(n // tt,),
        in_specs=[pl.BlockSpec(memory_space=pl.ANY),
                  pl.BlockSpec((tt, 2), lambda i, a, b: (i, 0))],
        out_specs=pl.BlockSpec((tt, 2 * d2), lambda i, a, b: (i, 0)),
        scratch_shapes=[pltpu.VMEM((2, tt, d2), jnp.uint32), pltpu.SemaphoreType.DMA(())])
    return pl.pallas_call(
        _moe_combine_kernel, grid_spec=grid_spec,
        out_shape=jax.ShapeDtypeStruct((n, 2 * d2), BF16),
        compiler_params=_params("arbitrary"),
    )(plan['d1'], plan['d2'], ys, wts_t)


def moe_sparse(h2p, ids, wts, cum, wg, wu, wd):
    n = h2p.shape[0]
    plan = moe_plan(ids, cum, n)
    xs = moe_gather(h2p, plan)
    ys = moe_ffn(xs, wg, wu, wd, plan)
    return moe_combine(ys, wts[:2].T, plan, n)


def kernel(x, c, ada_w, ada_b, norm1_g, w_in, gdn_conv_w, gdn_a_log, gdn_dt_bias, gdn_norm_g, gdn_w_out, conf_dw_w, conf_dw_b, conf_ln_g, conf_ln_b, conf_w_out, diff_q_norm_g, diff_k_norm_g, diff_lambda_q1, diff_lambda_k1, diff_lambda_q2, diff_lambda_k2, diff_sub_g, diff_w_out, w_o, norm2_g, router_w, router_bias, exp_w_gate, exp_w_up, exp_w_down):
    batch, seq, d = x.shape
    n = batch * seq
    depth = ada_w.shape[0]
    heads = d // D_PER_HEAD
    gw = heads * GDN_HEAD_DIM
    dw = heads * DIFF_V_DIM
    conf_ch = conf_dw_w.shape[-1]
    o_ba = 4 * gw
    o_conf = o_ba + 2 * heads
    o_diff = o_conf + 2 * conf_ch
    o_gate = o_diff + 3 * dw

    mod = ada_modulation(c, ada_w, ada_b)
    xf = x.reshape(n, d)
    y_prev = None
    mod4_prev = None
    for l in range(depth):
        mod4 = mod[l].reshape(batch, ADA_CHUNKS, 1, d)
        if y_prev is None:
            h = norm_modulate(xf, norm1_g[l], mod4, 1, 0, seq)
        else:
            xf, h = norm_modulate(xf, norm1_g[l], mod4, 1, 0, seq,
                                  y=y_prev, gate_mod4=mod4_prev, gate_chunk=5)
        wl = w_in[l]
        w_ba = jnp.zeros((d, LANES), BF16).at[:, :2 * heads].set(wl[:, o_ba:o_conf].astype(BF16))
        p_gdn = matmul(h, wl[:, :o_ba].astype(BF16), BF16).reshape(batch, seq, 4 * gw)
        ba = matmul(h, w_ba, F32).reshape(batch, seq, LANES)
        p_conf = matmul(h, wl[:, o_conf:o_diff].astype(BF16), BF16).reshape(batch, seq, 2 * conf_ch)
        p_diff = matmul(h, wl[:, o_diff:o_gate].astype(BF16), BF16)
        gates = matmul(h, wl[:, o_gate:].astype(BF16), BF16)

        u, w, qd, kt, qk, dec = gdn_prepare(p_gdn, ba, gdn_conv_w[l], gdn_a_log[l], gdn_dt_bias[l],
                                            batch, seq, heads)
        o_a = gdn_scan(u, w, qd, kt, qk, dec, p_gdn, gdn_norm_g[l], batch, seq, heads)
        o_b = conformer_conv(p_conf, conf_dw_w[l], conf_dw_b[l], conf_ln_g[l], conf_ln_b[l], batch, seq)
        qn, kn = diff_prepare(p_diff, diff_q_norm_g[l], diff_k_norm_g[l], heads)
        lam_init = 0.8 - 0.6 * math.exp(-0.3 * l)
        lam_vecs = jnp.stack([diff_lambda_q1[l], diff_lambda_k1[l], diff_lambda_q2[l], diff_lambda_k2[l]])
        o_c = diff_attention(qn.reshape(batch, seq, dw), kn.reshape(batch, seq, dw),
                             p_diff.reshape(batch, seq, 3 * dw), lam_vecs, diff_sub_g[l],
                             batch, seq, heads, lam_init)
        mixed = merge_branches(o_a.reshape(n, gw), o_b.reshape(n, conf_ch), o_c.reshape(n, dw),
                               gdn_w_out[l].astype(BF16), conf_w_out[l].astype(BF16),
                               diff_w_out[l].astype(BF16), gates, d)
        xf = project_residual(mixed, w_o[l].astype(BF16), xf, mod4, 2, seq)

        h2, ids, wts, cum = norm_route(xf, norm2_g[l], mod4, 4, 3, router_w, router_bias, seq)
        y_prev = moe_sparse(h2, ids, wts, cum, exp_w_gate[l].astype(BF16), exp_w_up[l].astype(BF16),
                            exp_w_down[l].astype(BF16))
        mod4_prev = mod4
    xf = gated_residual(xf, y_prev, mod4_prev, 5, seq)
    return xf.reshape(batch, seq, d)
```
